```python
import jax, jax.numpy as jnp
from jax import lax
import numpy as np

D_MODEL = 1024
BATCH = 16
SEQ = 4096
DEPTH = 2

EPS = 1e-6
MLA_HEADS = 8
MLA_NOPE = 64
MLA_ROPE = 32
MLA_V = 64
MLA_Q_RANK = 256
MLA_KV_RANK = 128
ROPE_BASE = 10000.0
ATTN_BLOCK = 128
CONV_WIDTH = D_MODEL // 2
CONV_K = 3
SG_GROUPS = 4
SG_WIDTH = D_MODEL // 2
SG_GROUP_DIM = SG_WIDTH // SG_GROUPS
SG_CHUNK = 128
GLA_HEADS = 4
GLA_DK = 64
GLA_DV = 128
GLA_GATE_RANK = 16
GLA_GATE_TAU = 16.0
GLA_CHUNK = 64
N_GROUPS = 4
EXPERTS_PER_GROUP = 8
N_EXPERTS = N_GROUPS * EXPERTS_PER_GROUP
TOP_K = 2
D_EXPERT = 256
MOE_BLOCK = 256

AB_SPLITS = (MLA_Q_RANK, MLA_KV_RANK, MLA_ROPE, CONV_WIDTH, CONV_WIDTH, CONV_WIDTH)
AB_IN = sum(AB_SPLITS)
AB_MIX = MLA_HEADS * MLA_V + CONV_WIDTH
CD_SPLITS = (SG_WIDTH, SG_WIDTH, GLA_HEADS * GLA_DK, GLA_HEADS * GLA_DK, GLA_HEADS * GLA_DV, GLA_GATE_RANK, GLA_HEADS * GLA_DV)
CD_IN = sum(CD_SPLITS)
CD_MIX = SG_WIDTH + GLA_HEADS * GLA_DV

kernel_name = "hybrid_mla_conv_gmlp_gla_hmoe"


def rmsnorm(x, g):
    xf = x.astype(jnp.float32)
    y = xf * lax.rsqrt(jnp.mean(xf * xf, axis=-1, keepdims=True) + EPS)
    return (y * g.astype(jnp.float32)).astype(x.dtype)


def split_cols(h, widths):
    return jnp.split(h, [int(v) for v in np.cumsum(widths)[:-1]], axis=-1)


def rope_tables(positions):
    inv = 1.0 / (ROPE_BASE ** (jnp.arange(0, MLA_ROPE, 2, dtype=jnp.float32) / MLA_ROPE))
    ang = positions.astype(jnp.float32)[..., None] * inv
    return jnp.cos(ang), jnp.sin(ang)


def apply_rope(x, cos, sin):
    half = x.shape[-1] // 2
    x1 = x[..., :half].astype(jnp.float32)
    x2 = x[..., half:].astype(jnp.float32)
    return jnp.concatenate([x1 * cos - x2 * sin, x1 * sin + x2 * cos], axis=-1).astype(x.dtype)


def mla_attention(q_lat, kv_lat, k_rope, q_norm_g, w_uq, kv_norm_g, w_ukv, cos, sin):
    B, S, _ = q_lat.shape
    q = (rmsnorm(q_lat, q_norm_g) @ w_uq).reshape(B, S, MLA_HEADS, MLA_NOPE + MLA_ROPE)
    q_pe = apply_rope(q[..., MLA_NOPE:], cos[:, :, None], sin[:, :, None])
    q = jnp.concatenate([q[..., :MLA_NOPE], q_pe], axis=-1)
    kv = (rmsnorm(kv_lat, kv_norm_g) @ w_ukv).reshape(B, S, MLA_HEADS, MLA_NOPE + MLA_V)
    k_pe = apply_rope(k_rope, cos, sin)
    k = jnp.concatenate([kv[..., :MLA_NOPE], jnp.broadcast_to(k_pe[:, :, None, :], (B, S, MLA_HEADS, MLA_ROPE))], axis=-1)
    qh = q.transpose(0, 2, 1, 3)
    kh = k.transpose(0, 2, 1, 3)
    vh = kv[..., MLA_NOPE:].transpose(0, 2, 1, 3)
    scale = (MLA_NOPE + MLA_ROPE) ** -0.5
    neg = jnp.finfo(jnp.float32).min
    outs = []
    for blk in range(S // ATTN_BLOCK):
        q0 = blk * ATTN_BLOCK
        end = q0 + ATTN_BLOCK
        s = jnp.einsum('bhqd,bhkd->bhqk', qh[:, :, q0:end], kh[:, :, :end]).astype(jnp.float32) * scale
        mask = (q0 + jnp.arange(ATTN_BLOCK))[:, None] >= jnp.arange(end)[None, :]
        p = jax.nn.softmax(jnp.where(mask, s, neg), axis=-1).astype(vh.dtype)
        outs.append(jnp.einsum('bhqk,bhkd->bhqd', p, vh[:, :, :end]))
    o = jnp.concatenate(outs, axis=2)
    return o.transpose(0, 2, 1, 3).reshape(B, S, MLA_HEADS * MLA_V)


def short_gated_conv(b_gate, c_gate, xin, conv_w):
    u = c_gate * xin
    y = lax.conv_general_dilated(u, conv_w[:, None, :].astype(u.dtype), window_strides=(1,),
                                 padding=[(CONV_K - 1, 0)], dimension_numbers=('NWC', 'WIO', 'NWC'),
                                 feature_group_count=CONV_WIDTH)
    return b_gate * y


def spatial_gating(u, v, v_norm_g, w_s, b_s):
    B, S, _ = v.shape
    n = S // SG_CHUNK
    vc = rmsnorm(v, v_norm_g).reshape(B, n, SG_CHUNK, SG_GROUPS, SG_GROUP_DIM)
    w = w_s * jnp.tril(jnp.ones((SG_CHUNK, SG_CHUNK), dtype=w_s.dtype))
    mixed = jnp.einsum('gts,bnsgd->bntgd', w, vc) + b_s.T[None, None, :, :, None]
    return u * mixed.reshape(B, S, SG_WIDTH)


def gated_linear_attention(q, k, v, g_low, r, w_g2, b_g, gn_g):
    B, S, _ = q.shape
    n = S // GLA_CHUNK
    f32 = jnp.float32
    log_a = jax.nn.log_sigmoid((g_low @ w_g2 + b_g).astype(f32)) / GLA_GATE_TAU

    def chunk(t, d):
        return t.astype(f32).reshape(B, n, GLA_CHUNK, GLA_HEADS, d).transpose(0, 3, 1, 2, 4)

    qc = chunk(q, GLA_DK) * (GLA_DK ** -0.5)
    kc = chunk(k, GLA_DK)
    vc = chunk(v, GLA_DV)
    bc = jnp.cumsum(chunk(log_a, GLA_DK), axis=3)
    b_last = bc[..., -1:, :]
    q_dec = qc * jnp.exp(bc)
    k_dec = kc * jnp.exp(-bc)
    k_to_end = kc * jnp.exp(b_last - bc)
    causal = jnp.tril(jnp.ones((GLA_CHUNK, GLA_CHUNK), dtype=bool))
    attn = jnp.where(causal, jnp.einsum('bhntd,bhnsd->bhnts', q_dec, k_dec), 0.0)
    o_intra = jnp.einsum('bhnts,bhnse->bhnte', attn, vc)
    chunk_kv = jnp.einsum('bhnsd,bhnse->bhnde', k_to_end, vc)
    decay = jnp.exp(b_last[..., 0, :])

    def step(state, inp):
        kv_n, dec_n = inp
        return state * dec_n[..., None] + kv_n, state

    init = jnp.zeros((B, GLA_HEADS, GLA_DK, GLA_DV), f32)
    _, states = lax.scan(step, init, (jnp.moveaxis(chunk_kv, 2, 0), jnp.moveaxis(decay, 2, 0)))
    states = jnp.moveaxis(states, 0, 2)
    o = o_intra + jnp.einsum('bhntd,bhnde->bhnte', q_dec, states)
    o = o.transpose(0, 2, 3, 1, 4).reshape(B, S, GLA_HEADS, GLA_DV)
    o = rmsnorm(o, gn_g.reshape(GLA_HEADS, GLA_DV)).reshape(B, S, GLA_HEADS * GLA_DV)
    return jax.nn.silu(r) * o.astype(r.dtype)


def hierarchical_moe(h, w_group, b_group, w_expert, b_expert, w_gate_up, w_down):
    B, S, D = h.shape
    N = B * S
    xt = h.reshape(N, D)
    f32 = jnp.float32
    grp_logits = (xt @ w_group).astype(f32) + b_group.astype(f32)
    grp_prob = jax.nn.softmax(grp_logits, axis=-1)
    _, g_sel = lax.top_k(grp_logits, 1)
    p_g = jnp.take_along_axis(grp_prob, g_sel, axis=-1)
    exp_logits = ((xt @ w_expert).astype(f32) + b_expert.astype(f32)).reshape(N, N_GROUPS, EXPERTS_PER_GROUP)
    in_grp = jnp.take_along_axis(exp_logits, g_sel[:, :, None], axis=1)[:, 0]
    top_val, top_loc = lax.top_k(in_grp, TOP_K)
    weights = p_g * jax.nn.softmax(top_val, axis=-1)
    expert_id = g_sel * EXPERTS_PER_GROUP + top_loc

    NK = N * TOP_K
    flat_e = expert_id.reshape(NK)
    flat_tok = jnp.repeat(jnp.arange(N, dtype=jnp.int32), TOP_K)
    flat_w = weights.reshape(NK)
    order = jnp.argsort(flat_e)
    se = flat_e[order]
    counts = jnp.bincount(flat_e, length=N_EXPERTS)
    padded = (counts + MOE_BLOCK - 1) // MOE_BLOCK * MOE_BLOCK
    pad_end = jnp.cumsum(padded)
    pad_start = pad_end - padded
    start = jnp.cumsum(counts) - counts
    dest = pad_start[se] + jnp.arange(NK, dtype=jnp.int32) - start[se]
    n_blocks = (NK + MOE_BLOCK - 1) // MOE_BLOCK + N_EXPERTS
    n_rows = n_blocks * MOE_BLOCK
    row_tok = jnp.full((n_rows,), N, jnp.int32).at[dest].set(flat_tok[order])
    row_w = jnp.zeros((n_rows,), f32).at[dest].set(flat_w[order])
    blk_expert = jnp.minimum(jnp.searchsorted(pad_end, jnp.arange(n_blocks) * MOE_BLOCK, side='right'), N_EXPERTS - 1)
    x_pad = jnp.concatenate([xt, jnp.zeros((1, D), xt.dtype)], axis=0)
    xb = x_pad[row_tok].reshape(n_blocks, MOE_BLOCK, D)

    def expert_block(args):
        xblk, e = args
        gu = xblk @ w_gate_up[e]
        return (jax.nn.silu(gu[:, :D_EXPERT]) * gu[:, D_EXPERT:]) @ w_down[e]

    yb = lax.map(expert_block, (xb, blk_expert)).reshape(n_rows, D)
    y = jax.ops.segment_sum(yb * row_w[:, None].astype(yb.dtype), row_tok, num_segments=N + 1)[:N]
    return y.reshape(B, S, D)


def setup_inputs(seed: int = 0) -> dict:
    key = jax.random.key(seed)
    ks = iter(jax.random.split(key, 32))
    n_even = (DEPTH + 1) // 2
    n_odd = DEPTH // 2
    D = D_MODEL

    def nrm(shape, scale):
        return jax.random.normal(next(ks), shape, jnp.float32) * scale

    def gain(shape):
        return 1.0 + nrm(shape, 0.02)

    x = nrm((BATCH, SEQ, D), 1.0)
    c = nrm((BATCH, D), 1.0)
    offs = jax.random.randint(next(ks), (BATCH, 1), 0, 2048)
    positions = (offs + jnp.arange(SEQ)[None, :]).astype(jnp.int32)
    return {
        "x": x, "c": c, "positions": positions,
        "adaln_w": nrm((DEPTH, D, 6 * D), 0.5 * D ** -0.5),
        "adaln_b": nrm((DEPTH, 6 * D), 0.02),
        "norm_mix_g": gain((DEPTH, D)),
        "norm_ffn_g": gain((DEPTH, D)),
        "ab_w_in": nrm((n_even, D, AB_IN), D ** -0.5),
        "mla_q_norm_g": gain((n_even, MLA_Q_RANK)),
        "mla_w_uq": nrm((n_even, MLA_Q_RANK, MLA_HEADS * (MLA_NOPE + MLA_ROPE)), MLA_Q_RANK ** -0.5),
        "mla_kv_norm_g": gain((n_even, MLA_KV_RANK)),
        "mla_w_ukv": nrm((n_even, MLA_KV_RANK, MLA_HEADS * (MLA_NOPE + MLA_V)), MLA_KV_RANK ** -0.5),
        "conv_w": nrm((n_even, CONV_K, CONV_WIDTH), CONV_K ** -0.5),
        "ab_w_out": nrm((n_even, AB_MIX, D), AB_MIX ** -0.5),
        "cd_w_in": nrm((n_odd, D, CD_IN), D ** -0.5),
        "sg_v_norm_g": gain((n_odd, SG_WIDTH)),
        "sg_w_s": nrm((n_odd, SG_GROUPS, SG_CHUNK, SG_CHUNK), SG_CHUNK ** -0.5),
        "sg_b_s": 1.0 + nrm((n_odd, SG_GROUPS, SG_CHUNK), 0.1),
        "gla_w_g2": nrm((n_odd, GLA_GATE_RANK, GLA_HEADS * GLA_DK), GLA_GATE_RANK ** -0.5),
        "gla_b_g": nrm((n_odd, GLA_HEADS * GLA_DK), 0.02),
        "gla_norm_g": gain((n_odd, GLA_HEADS * GLA_DV)),
        "cd_w_out": nrm((n_odd, CD_MIX, D), CD_MIX ** -0.5),
        "moe_w_group": nrm((DEPTH, D, N_GROUPS), D ** -0.5),
        "moe_b_group": nrm((DEPTH, N_GROUPS), 0.01),
        "moe_w_expert": nrm((DEPTH, D, N_EXPERTS), D ** -0.5),
        "moe_b_expert": nrm((DEPTH, N_EXPERTS), 0.01),
        "moe_w_gate_up": nrm((DEPTH, N_EXPERTS, D, 2 * D_EXPERT), D ** -0.5),
        "moe_w_down": nrm((DEPTH, N_EXPERTS, D_EXPERT, D), D_EXPERT ** -0.5),
        "final_norm_g": gain((D,)),
    }


def reference(x, c, positions, adaln_w, adaln_b, norm_mix_g, norm_ffn_g,
              ab_w_in, mla_q_norm_g, mla_w_uq, mla_kv_norm_g, mla_w_ukv, conv_w, ab_w_out,
              cd_w_in, sg_v_norm_g, sg_w_s, sg_b_s, gla_w_g2, gla_b_g, gla_norm_g, cd_w_out,
              moe_w_group, moe_b_group, moe_w_expert, moe_b_expert, moe_w_gate_up, moe_w_down,
              final_norm_g):
    cos, sin = rope_tables(positions)
    c_act = jax.nn.silu(c)
    for l in range(DEPTH):
        mod = (c_act @ adaln_w[l] + adaln_b[l])[:, None, :]
        sh1, sc1, g1, sh2, sc2, g2 = jnp.split(mod, 6, axis=-1)
        h = rmsnorm(x, norm_mix_g[l]) * (1.0 + sc1) + sh1
        i = l // 2
        if l % 2 == 0:
            q_lat, kv_lat, k_rope, b_gate, c_gate, xin = split_cols(h @ ab_w_in[i], AB_SPLITS)
            a_out = mla_attention(q_lat, kv_lat, k_rope, mla_q_norm_g[i], mla_w_uq[i],
                                  mla_kv_norm_g[i], mla_w_ukv[i], cos, sin)
            b_out = short_gated_conv(b_gate, c_gate, xin, conv_w[i])
            mix = jnp.concatenate([a_out, b_out], axis=-1) @ ab_w_out[i]
        else:
            proj = h @ cd_w_in[i]
            u, v, gq, gk, gv, g_low, gr = split_cols(proj, CD_SPLITS)
            c_out = spatial_gating(jax.nn.gelu(u), jax.nn.gelu(v), sg_v_norm_g[i], sg_w_s[i], sg_b_s[i])
            d_out = gated_linear_attention(gq, gk, gv, g_low, gr, gla_w_g2[i], gla_b_g[i], gla_norm_g[i])
            mix = jnp.concatenate([c_out, d_out], axis=-1) @ cd_w_out[i]
        x = x + g1 * mix
        h = rmsnorm(x, norm_ffn_g[l]) * (1.0 + sc2) + sh2
        x = x + g2 * hierarchical_moe(h, moe_w_group[l], moe_b_group[l], moe_w_expert[l],
                                      moe_b_expert[l], moe_w_gate_up[l], moe_w_down[l])
    return rmsnorm(x, final_norm_g)
```

```python
import functools

import jax
import jax.numpy as jnp
from jax import lax
from jax.experimental import pallas as pl
from jax.experimental.pallas import tpu as pltpu

F32 = jnp.float32
BF16 = jnp.bfloat16
HIGHEST = lax.Precision.HIGHEST

EPS = 1e-6
MLA_HEADS = 8
MLA_NOPE = 64
MLA_ROPE = 32
MLA_V = 64
MLA_Q_RANK = 256
MLA_KV_RANK = 128
ROPE_BASE = 10000.0
HEAD_PAD = 128
CONV_K = 3
SG_GROUPS = 4
SG_CHUNK = 128
GLA_HEADS = 4
GLA_DK = 64
GLA_DV = 128
GLA_GATE_RANK = 16
GLA_GATE_TAU = 16.0
GLA_CHUNK = 64
N_GROUPS = 4
EXPERTS_PER_GROUP = 8
N_EXPERTS = N_GROUPS * EXPERTS_PER_GROUP
D_EXPERT = 256
MOE_TILE = 2048
MOE_CHUNK = 128
LANES = 128
ROUTE_LANES = 128
EXPERT_LANE0 = N_GROUPS

ROW_TILE = 512
ATTN_TQ = 256
ATTN_TK = 256
VMEM_LIMIT = 56 * 1024 * 1024


def _rms(x, g):
    return x * lax.rsqrt(jnp.mean(x * x, axis=-1, keepdims=True) + EPS) * g


def _silu(x):
    return x * jax.nn.sigmoid(x)


def _gelu_tanh(x):
    return 0.5 * x * (1.0 + jnp.tanh(0.7978845608028654 * (x + 0.044715 * (x * x * x))))


def _dot(a, b):
    return jnp.dot(a, b, preferred_element_type=F32)


def _params(sem):
    return pltpu.CompilerParams(dimension_semantics=sem, vmem_limit_bytes=VMEM_LIMIT)


def _to_lin(ref, val, lead=()):
    r, d = val.shape
    nb = d // LANES
    for jb in range(nb):
        ref[lead + (pl.ds(jb, r, stride=nb), slice(None))] = val[:, jb * LANES:(jb + 1) * LANES]


def _from_lin(ref, r, nb, lead=()):
    return jnp.concatenate([ref[lead + (pl.ds(jb, r, stride=nb), slice(None))] for jb in range(nb)], axis=1)


def _adaln_kernel(c_ref, w_ref, b_ref, o_ref):
    c = c_ref[...]
    o_ref[0] = jnp.dot(_silu(c), w_ref[0], preferred_element_type=F32, precision=HIGHEST) + b_ref[0]


def _adaln_mod(c, adaln_w, adaln_b):
    depth, d, d6 = adaln_w.shape
    b = c.shape[0]
    n_chunks = d6 // d
    out = pl.pallas_call(
        _adaln_kernel,
        grid=(depth, n_chunks),
        in_specs=[
            pl.BlockSpec((b, d), lambda l, j: (0, 0)),
            pl.BlockSpec((1, d, d), lambda l, j: (l, 0, j)),
            pl.BlockSpec((1, 1, d), lambda l, j: (l, 0, j)),
        ],
        out_specs=pl.BlockSpec((1, b, d), lambda l, j: (l, 0, j)),
        out_shape=jax.ShapeDtypeStruct((depth, b, d6), F32),
        compiler_params=_params(("arbitrary", "arbitrary")),
        name="adaln_mod",
    )(c, adaln_w, adaln_b.reshape(depth, 1, d6))
    return out.reshape(depth, b, n_chunks, d)


def _rope(blk, c, s1, s2):
    return blk * c + pltpu.roll(blk, HEAD_PAD - MLA_ROPE // 2, 1) * s1 + pltpu.roll(blk, MLA_ROPE // 2, 1) * s2


def _l0_in_kernel(x_ref, mod_ref, g_ref, win_ref, qg_ref, wuq_ref, kvg_ref, wuk_ref, wuv_ref,
                  rc_ref, rs1_ref, rs2_ref, cw_ref,
                  q_ref, k_ref, v_ref, bc_ref, ubuf):
    j = pl.program_id(1)
    tm = x_ref.shape[1]
    x = x_ref[0]
    m = mod_ref[0]
    h = _rms(x, g_ref[...]) * (1.0 + m[1:2]) + m[0:1]
    proj = _dot(h.astype(BF16), win_ref[...])
    q_lat = proj[:, 0:256]
    kv_lat = proj[:, 256:384]
    kr = proj[:, 384:512]
    bg = proj[:, 512:1024]
    cg = proj[:, 1024:1536]
    xi = proj[:, 1536:2048]

    c = rc_ref[0]
    s1 = rs1_ref[0]
    s2 = rs2_ref[0]
    scale = (MLA_NOPE + MLA_ROPE) ** -0.5
    q = _dot(_rms(q_lat, qg_ref[...]).astype(BF16), wuq_ref[...])
    kvn = _rms(kv_lat, kvg_ref[...]).astype(BF16)
    kn = _dot(kvn, wuk_ref[...])
    v_ref[0] = _dot(kvn, wuv_ref[...]).astype(BF16)
    krr = _rope(kr, c, s1, s2)
    for hd in range(MLA_HEADS):
        sl = slice(hd * HEAD_PAD, (hd + 1) * HEAD_PAD)
        q_ref[0, :, sl] = (_rope(q[:, sl], c, s1, s2) * scale).astype(BF16)
        k_ref[0, :, sl] = (kn[:, sl] + krr).astype(BF16)

    u = cg * xi

    @pl.when(j == 0)
    def _():
        ubuf[0:8, :] = jnp.zeros((8, ubuf.shape[1]), F32)

    ubuf[8:tm + 8, :] = u
    r1 = ubuf[7:tm + 7, :]
    r2 = ubuf[6:tm + 6, :]
    cw = cw_ref[...]
    y = cw[2:3] * u + cw[1:2] * r1 + cw[0:1] * r2
    bc_ref[0] = (bg * y).astype(BF16)
    ubuf[0:8, :] = u[tm - 8:tm, :]


def _l0_in(x, mod, g, win, qg, wuq, kvg, wuk, wuv, rc, rs1, rs2, cw):
    b, s, d = x.shape
    tm = ROW_TILE
    full = lambda a: pl.BlockSpec(a.shape, lambda i, j: (0,) * a.ndim)
    row = lambda w: pl.BlockSpec((1, tm, w), lambda i, j: (i, j, 0))
    return pl.pallas_call(
        _l0_in_kernel,
        grid=(b, s // tm),
        in_specs=[row(d), pl.BlockSpec((1, 6, d), lambda i, j: (i, 0, 0)), full(g), full(win), full(qg), full(wuq),
                  full(kvg), full(wuk), full(wuv), row(HEAD_PAD), row(HEAD_PAD), row(HEAD_PAD), full(cw)],
        out_specs=[row(MLA_HEADS * HEAD_PAD), row(MLA_HEADS * HEAD_PAD), row(MLA_HEADS * MLA_V), row(d // 2)],
        out_shape=[jax.ShapeDtypeStruct((b, s, MLA_HEADS * HEAD_PAD), BF16),
                   jax.ShapeDtypeStruct((b, s, MLA_HEADS * HEAD_PAD), BF16),
                   jax.ShapeDtypeStruct((b, s, MLA_HEADS * MLA_V), BF16),
                   jax.ShapeDtypeStruct((b, s, d // 2), BF16)],
        scratch_shapes=[pltpu.VMEM((tm + 8, d // 2), F32)],
        compiler_params=_params(("arbitrary", "arbitrary")),
        name="l0_in",
    )(x, mod, g, win, qg, wuq, kvg, wuk, wuv, rc, rs1, rs2, cw)


def _attn_kernel(q_ref, k_ref, v_ref, o_ref):
    qi = pl.program_id(2)
    tq = q_ref.shape[1]
    tk = ATTN_TK
    neg = jnp.finfo(F32).min
    qs = [q_ref[0, :, 0:HEAD_PAD], q_ref[0, :, HEAD_PAD:2 * HEAD_PAD]]

    def step(kblk, vblk, carry, mask):
        out = []
        for hd in range(2):
            m, l, acc = carry[hd]
            s = lax.dot_general(qs[hd], kblk[:, hd * HEAD_PAD:(hd + 1) * HEAD_PAD],
                                (((1,), (1,)), ((), ())), preferred_element_type=F32)
            if mask is not None:
                s = jnp.where(mask, s, neg)
            m_new = jnp.maximum(m, jnp.max(s, axis=-1, keepdims=True))
            alpha = jnp.exp(m - m_new)
            p = jnp.exp(s - m_new)
            l = alpha * l + jnp.sum(p, axis=-1, keepdims=True)
            acc = alpha * acc + _dot(p.astype(BF16), vblk)
            out.append((m_new, l, acc))
        return tuple(out)

    def body(jj, carry):
        off = pl.multiple_of(jj * tk, tk)
        return step(k_ref[0, pl.ds(off, tk), :], v_ref[0, pl.ds(off, tk), :], carry, None)

    init = tuple((jnp.full((tq, 1), neg, F32), jnp.zeros((tq, 1), F32), jnp.zeros((tq, 2 * MLA_V), F32))
                 for _ in range(2))
    carry = lax.fori_loop(0, qi * (tq // tk), body, init)
    rows = lax.broadcasted_iota(jnp.int32, (tq, tk), 0)
    cols = lax.broadcasted_iota(jnp.int32, (tq, tk), 1)
    for d in range(tq // tk):
        off = pl.multiple_of(qi * tq + d * tk, tk)
        carry = step(k_ref[0, pl.ds(off, tk), :], v_ref[0, pl.ds(off, tk), :], carry, rows >= cols + d * tk)
    lane = lax.broadcasted_iota(jnp.int32, (tq, 2 * MLA_V), 1)
    o = jnp.where(lane < MLA_V, carry[0][2] / carry[0][1], carry[1][2] / carry[1][1])
    o_ref[0] = o.astype(BF16)


def _attention(q, k, v):
    b, s, _ = q.shape
    tq = ATTN_TQ
    return pl.pallas_call(
        _attn_kernel,
        grid=(b, MLA_HEADS // 2, s // tq),
        in_specs=[pl.BlockSpec((1, tq, 2 * HEAD_PAD), lambda i, h, j: (i, j, h)),
                  pl.BlockSpec((1, s, 2 * HEAD_PAD), lambda i, h, j: (i, 0, h)),
                  pl.BlockSpec((1, s, 2 * MLA_V), lambda i, h, j: (i, 0, h))],
        out_specs=pl.BlockSpec((1, tq, 2 * MLA_V), lambda i, h, j: (i, j, h)),
        out_shape=jax.ShapeDtypeStruct((b, s, MLA_HEADS * MLA_V), BF16),
        compiler_params=_params(("arbitrary", "arbitrary", "arbitrary")),
        name="mla_attention",
    )(q, k, v)


def _outproj_router_kernel(a_ref, b_ref, x_ref, mod_ref, wa_ref, wb_ref, g_ref, wr_ref, br_ref,
                           x1_ref, h2_ref, route_ref, cnt_ref, carry):
    i = pl.program_id(0)
    tm = x_ref.shape[0]
    m = mod_ref[0]
    mix = _dot(a_ref[...], wa_ref[...]) + _dot(b_ref[...], wb_ref[...])
    x1 = x_ref[...] + m[2:3] * mix
    x1_ref[...] = x1
    h2 = _rms(x1, g_ref[...]) * (1.0 + m[4:5]) + m[3:4]
    _to_lin(h2_ref, h2)
    logits = _dot(h2.astype(BF16), wr_ref[...]) + br_ref[...]

    neg = jnp.finfo(F32).min
    lane = lax.broadcasted_iota(jnp.int32, (tm, ROUTE_LANES), 1).astype(F32)
    big = float(ROUTE_LANES)
    is_grp = lane < N_GROUPS
    lg = jnp.where(is_grp, logits, neg)
    gmax = jnp.max(lg, axis=-1, keepdims=True)
    gsel = jnp.min(jnp.where(lg == gmax, lane, big), axis=-1, keepdims=True)
    pg = 1.0 / jnp.sum(jnp.where(is_grp, jnp.exp(logits - gmax), 0.0), axis=-1, keepdims=True)
    lo = EXPERT_LANE0 + EXPERTS_PER_GROUP * gsel
    le = jnp.where((lane >= lo) & (lane < lo + EXPERTS_PER_GROUP), logits, neg)
    v0 = jnp.max(le, axis=-1, keepdims=True)
    l0 = jnp.min(jnp.where(le == v0, lane, big), axis=-1, keepdims=True)
    le2 = jnp.where(lane == l0, neg, le)
    v1 = jnp.max(le2, axis=-1, keepdims=True)
    l1 = jnp.min(jnp.where(le2 == v1, lane, big), axis=-1, keepdims=True)
    t = jnp.exp(v1 - v0)
    w0 = pg / (1.0 + t)
    w1 = pg * t / (1.0 + t)

    @pl.when(i % (MOE_TILE // tm) == 0)
    def _():
        carry[...] = jnp.zeros_like(carry)

    sel0 = lane == l0
    sel1 = lane == l1
    oh = jnp.where(sel0 | sel1, 1.0, 0.0)
    r = lax.broadcasted_iota(jnp.int32, (tm, tm), 0)
    cc = lax.broadcasted_iota(jnp.int32, (tm, tm), 1)
    stril = jnp.where(r > cc, 1.0, 0.0).astype(BF16)
    cum = _dot(stril, oh.astype(BF16)) + carry[...]
    rank0 = jnp.sum(jnp.where(sel0, cum, 0.0), axis=-1, keepdims=True)
    rank1 = jnp.sum(jnp.where(sel1, cum, 0.0), axis=-1, keepdims=True)
    carry[...] = carry[...] + jnp.sum(oh, axis=0, keepdims=True)
    cnt_ref[0] = carry[...]

    rec = jnp.where(lane == 0, l0 - EXPERT_LANE0, 0.0)
    rec = jnp.where(lane == 1, l1 - EXPERT_LANE0, rec)
    rec = jnp.where(lane == 2, rank0, rec)
    rec = jnp.where(lane == 3, rank1, rec)
    rec = jnp.where(lane == 4, w0, rec)
    rec = jnp.where(lane == 5, w1, rec)
    route_ref[...] = rec


def _outproj_router(a, bb, x, mod, wa, wb, g, wr, br, tiles_per_seq):
    n, d = x.shape
    tm = ROW_TILE
    nb = d // LANES
    full = lambda arr: pl.BlockSpec(arr.shape, lambda i: (0,) * arr.ndim)
    row = lambda w: pl.BlockSpec((tm, w), lambda i: (i, 0))
    return pl.pallas_call(
        _outproj_router_kernel,
        grid=(n // tm,),
        in_specs=[row(a.shape[1]), row(bb.shape[1]), row(d),
                  pl.BlockSpec((1, 6, d), lambda i: (i // tiles_per_seq, 0, 0)),
                  full(wa), full(wb), full(g), full(wr), full(br)],
        out_specs=[row(d), pl.BlockSpec((tm * nb, LANES), lambda i: (i, 0)), row(ROUTE_LANES),
                   pl.BlockSpec((1, 1, ROUTE_LANES), lambda i: (i // (MOE_TILE // tm), 0, 0))],
        out_shape=[jax.ShapeDtypeStruct((n, d), F32), jax.ShapeDtypeStruct((n * nb, LANES), F32),
                   jax.ShapeDtypeStruct((n, ROUTE_LANES), F32),
                   jax.ShapeDtypeStruct((n // MOE_TILE, 1, ROUTE_LANES), F32)],
        scratch_shapes=[pltpu.VMEM((1, ROUTE_LANES), F32)],
        compiler_params=_params(("arbitrary",)),
        name="outproj_router",
    )(a, bb, x, mod, wa, wb, g, wr, br)


def _moe_kernel(ls_ref, pc_ref, idx_ref, w_ref, h_ref, wgu_ref, wd_ref, o_ref, xbuf, ybuf):
    i = pl.program_id(0)
    e = pl.program_id(1)
    nb = xbuf.shape[0] // MOE_CHUNK
    last_tok = h_ref.shape[0] // nb - 1

    @pl.when(e == 0)
    def _():
        o_ref[0] = jnp.zeros(o_ref.shape[1:], F32)

    start = ls_ref[i * N_EXPERTS + e]
    cnt = pc_ref[i * N_EXPERTS + e]

    def chunk(c, carry):
        base = start + c * MOE_CHUNK
        for r in range(MOE_CHUNK):
            t = jnp.minimum(idx_ref[0, 0, base + r], last_tok)
            xbuf[r * nb:(r + 1) * nb, :] = h_ref[pl.ds(pl.multiple_of(t * nb, nb), nb), :]
        x = _from_lin(xbuf, MOE_CHUNK, nb).astype(BF16)
        gu = _dot(x, wgu_ref[0])
        hmid = _silu(gu[:, :D_EXPERT]) * gu[:, D_EXPERT:]
        _to_lin(ybuf, _dot(hmid.astype(BF16), wd_ref[0]))

        def group(g, cc):
            rows = []
            for k in range(8):
                r = g * 8 + k
                dst = pl.multiple_of(idx_ref[0, 0, base + r] * nb, nb)
                src = pl.multiple_of(r * nb, nb)
                rows.append((dst, o_ref[0, pl.ds(dst, nb), :] + w_ref[0, 0, base + r] * ybuf[pl.ds(src, nb), :]))
            for dst, val in rows:
                o_ref[0, pl.ds(dst, nb), :] = val
            return cc

        n_groups = lax.shift_right_logical(jnp.minimum(cnt - c * MOE_CHUNK, MOE_CHUNK), 3)
        lax.fori_loop(0, n_groups, group, 0)
        return carry

    lax.fori_loop(0, lax.shift_right_logical(cnt + (MOE_CHUNK - 1), 7), chunk, 0)


def _moe_ffn(h2lin, lstart, padded, idx, ws, wgu, wd):
    n_tiles, _, plan_len = idx.shape
    d = wgu.shape[1]
    nb = d // LANES
    smem = lambda: pl.BlockSpec((1, 1, plan_len), lambda i, e, ls, pc: (i, 0, 0), memory_space=pltpu.SMEM)
    out_rows = (MOE_TILE + 8) * nb
    grid_spec = pltpu.PrefetchScalarGridSpec(
        num_scalar_prefetch=2,
        grid=(n_tiles, N_EXPERTS),
        in_specs=[smem(), smem(),
                  pl.BlockSpec((MOE_TILE * nb, LANES), lambda i, e, ls, pc: (i, 0)),
                  pl.BlockSpec((1, d, 2 * D_EXPERT), lambda i, e, ls, pc: (e, 0, 0)),
                  pl.BlockSpec((1, D_EXPERT, d), lambda i, e, ls, pc: (e, 0, 0))],
        out_specs=pl.BlockSpec((1, out_rows, LANES), lambda i, e, ls, pc: (i, 0, 0)),
        scratch_shapes=[pltpu.VMEM((MOE_CHUNK * nb, LANES), F32), pltpu.VMEM((MOE_CHUNK * nb, LANES), F32)],
    )
    return pl.pallas_call(
        _moe_kernel,
        grid_spec=grid_spec,
        out_shape=jax.ShapeDtypeStruct((n_tiles, out_rows, LANES), F32),
        compiler_params=_params(("arbitrary", "arbitrary")),
        name="moe_ffn",
    )(lstart, padded, idx, ws, h2lin, wgu, wd)


def _route_plan(route, cnt):
    n_tiles = cnt.shape[0]
    pairs = 2 * MOE_TILE
    e = route[:, 0:2].astype(jnp.int32).reshape(n_tiles, pairs)
    rank = route[:, 2:4].astype(jnp.int32).reshape(n_tiles, pairs)
    w = route[:, 4:6].reshape(n_tiles, pairs)
    counts = cnt[:, 0, EXPERT_LANE0:EXPERT_LANE0 + N_EXPERTS].astype(jnp.int32)
    padded = (counts + 7) // 8 * 8
    lstart = jnp.cumsum(padded, axis=1) - padded
    dest = jnp.take_along_axis(lstart, e, axis=1) + rank
    plan_len = pairs + 8 * N_EXPERTS + MOE_CHUNK
    tile_ix = jnp.arange(n_tiles, dtype=jnp.int32)[:, None]
    tok = jnp.broadcast_to(jnp.arange(pairs, dtype=jnp.int32) // 2, (n_tiles, pairs))
    idx = jnp.full((n_tiles, plan_len), MOE_TILE, jnp.int32).at[tile_ix, dest].set(tok)
    ws = jnp.zeros((n_tiles, plan_len), F32).at[tile_ix, dest].set(w)
    return lstart.reshape(-1), padded.reshape(-1), idx[:, None, :], ws[:, None, :]


def _moe(h2lin, route, cnt, wgu, wd):
    lstart, padded, idx, ws = _route_plan(route, cnt)
    return _moe_ffn(h2lin, lstart, padded, idx, ws, wgu, wd)


def _moe_spec(tm, nb, s):
    per = MOE_TILE // tm
    return pl.BlockSpec((1, tm * nb, LANES), lambda i, j: (i * (s // MOE_TILE) + j // per, j % per, 0))


def _l1_in_kernel(x_ref, mo_ref, mod0_ref, mod1_ref, g_ref, win_ref, vg_ref, ws_ref, bst_ref,
                  wg2_ref, bg_ref,
                  x2_ref, c_ref, q_ref, k_ref, v_ref, r_ref, la_ref):
    tm = x_ref.shape[1]
    m1 = mod1_ref[0]
    x2 = x_ref[0] + mod0_ref[0][5:6] * _from_lin(mo_ref, tm, x_ref.shape[2] // LANES, (0,))
    x2_ref[0] = x2
    h = _rms(x2, g_ref[...]) * (1.0 + m1[1:2]) + m1[0:1]
    proj = _dot(h.astype(BF16), win_ref[...])
    u = _gelu_tanh(proj[:, 0:512])
    vn = _rms(_gelu_tanh(proj[:, 512:1024]), vg_ref[...]).astype(BF16)
    q_ref[0] = proj[:, 1024:1280].astype(BF16)
    k_ref[0] = proj[:, 1280:1536].astype(BF16)
    v_ref[0] = proj[:, 1536:2048].astype(BF16)
    r_ref[0] = proj[:, 2048:2560].astype(BF16)
    z = _dot(proj[:, 2560:2688].astype(BF16), wg2_ref[...]) + bg_ref[...]
    la_ref[0] = (jnp.minimum(z, 0.0) - jnp.log(1.0 + jnp.exp(-jnp.abs(z)))) * (1.0 / GLA_GATE_TAU)

    rr = lax.broadcasted_iota(jnp.int32, (SG_CHUNK, SG_CHUNK), 0)
    cc = lax.broadcasted_iota(jnp.int32, (SG_CHUNK, SG_CHUNK), 1)
    gd = 512 // SG_GROUPS
    for g in range(SG_GROUPS):
        wt = jnp.where(rr >= cc, ws_ref[g], 0.0).astype(BF16)
        bcol = bst_ref[:, g:g + 1]
        for ch in range(tm // SG_CHUNK):
            rows = slice(ch * SG_CHUNK, (ch + 1) * SG_CHUNK)
            cols = slice(g * gd, (g + 1) * gd)
            mixed = _dot(wt, vn[rows, cols]) + bcol
            c_ref[0, rows, cols] = (u[rows, cols] * mixed).astype(BF16)


def _l1_in(x, mo, mod0, mod1, g, win, vg, ws, bst, wg2, bg):
    b, s, d = x.shape
    tm = ROW_TILE
    full = lambda a: pl.BlockSpec(a.shape, lambda i, j: (0,) * a.ndim)
    row = lambda w: pl.BlockSpec((1, tm, w), lambda i, j: (i, j, 0))
    modspec = pl.BlockSpec((1, 6, d), lambda i, j: (i, 0, 0))
    widths = [d, 512, GLA_HEADS * GLA_DK, GLA_HEADS * GLA_DK, GLA_HEADS * GLA_DV, GLA_HEADS * GLA_DV,
              GLA_HEADS * GLA_DK]
    dtypes = [F32, BF16, BF16, BF16, BF16, BF16, F32]
    return pl.pallas_call(
        _l1_in_kernel,
        grid=(b, s // tm),
        in_specs=[row(d), _moe_spec(tm, d // LANES, s), modspec, modspec, full(g), full(win), full(vg), full(ws),
                  full(bst), full(wg2), full(bg)],
        out_specs=[row(w) for w in widths],
        out_shape=[jax.ShapeDtypeStruct((b, s, w), dt) for w, dt in zip(widths, dtypes)],
        compiler_params=_params(("arbitrary", "arbitrary")),
        name="l1_in",
    )(x, mo, mod0, mod1, g, win, vg, ws, bst, wg2, bg)


def _gla_kernel(q_ref, k_ref, v_ref, r_ref, la_ref, gn_ref, o_ref, state):
    j = pl.program_id(1)
    tm = q_ref.shape[1]
    ck = GLA_CHUNK

    @pl.when(j == 0)
    def _():
        state[...] = jnp.zeros_like(state)

    rr = lax.broadcasted_iota(jnp.int32, (ck, ck), 0)
    cc = lax.broadcasted_iota(jnp.int32, (ck, ck), 1)
    causal = rr >= cc
    tril = jnp.where(causal, 1.0, 0.0)
    for ch in range(tm // ck):
        rows = slice(ch * ck, (ch + 1) * ck)
        la = la_ref[0, rows, :]
        bc = jnp.dot(tril, la, preferred_element_type=F32, precision=HIGHEST)
        b_last = bc[ck - 1:ck, :]
        q = q_ref[0, rows, :].astype(F32) * (GLA_DK ** -0.5)
        k = k_ref[0, rows, :].astype(F32)
        q_dec = (q * jnp.exp(bc)).astype(BF16)
        k_dec = (k * jnp.exp(-bc)).astype(BF16)
        k_end = k * jnp.exp(b_last - bc)
        decay = jnp.exp(b_last)
        for hd in range(GLA_HEADS):
            ks = slice(hd * GLA_DK, (hd + 1) * GLA_DK)
            vs = slice(hd * GLA_DV, (hd + 1) * GLA_DV)
            vh = v_ref[0, rows, vs]
            attn = lax.dot_general(q_dec[:, ks], k_dec[:, ks], (((1,), (1,)), ((), ())),
                                   preferred_element_type=F32)
            attn = jnp.where(causal, attn, 0.0)
            st = state[hd]
            o = _dot(attn.astype(BF16), vh) + lax.dot_general(
                q_dec[:, ks], st.astype(BF16), (((1,), (1,)), ((), ())), preferred_element_type=F32)
            kv = lax.dot_general(vh, k_end[:, ks].astype(BF16), (((0,), (0,)), ((), ())),
                                 preferred_element_type=F32)
            state[hd] = st * decay[:, ks] + kv
            o = _rms(o, gn_ref[:, vs])
            o_ref[0, rows, vs] = (_silu(r_ref[0, rows, vs].astype(F32)) * o).astype(BF16)


def _gla(q, k, v, r, la, gn):
    b, s, _ = q.shape
    tm = ROW_TILE
    row = lambda w: pl.BlockSpec((1, tm, w), lambda i, j: (i, j, 0))
    return pl.pallas_call(
        _gla_kernel,
        grid=(b, s // tm),
        in_specs=[row(q.shape[2]), row(k.shape[2]), row(v.shape[2]), row(r.shape[2]), row(la.shape[2]),
                  pl.BlockSpec(gn.shape, lambda i, j: (0, 0))],
        out_specs=row(v.shape[2]),
        out_shape=jax.ShapeDtypeStruct(v.shape, BF16),
        scratch_shapes=[pltpu.VMEM((GLA_HEADS, GLA_DV, GLA_DK), F32)],
        compiler_params=_params(("arbitrary", "arbitrary")),
        name="gla",
    )(q, k, v, r, la, gn)


def _final_kernel(x_ref, mo_ref, mod_ref, g_ref, o_ref):
    tm, d = x_ref.shape[1], x_ref.shape[2]
    x = x_ref[0] + mod_ref[0][5:6] * _from_lin(mo_ref, tm, d // LANES, (0,))
    o_ref[0] = _rms(x, g_ref[...])


def _final(x, mo, mod, g):
    b, s, d = x.shape
    tm = ROW_TILE
    row = lambda w: pl.BlockSpec((1, tm, w), lambda i, j: (i, j, 0))
    return pl.pallas_call(
        _final_kernel,
        grid=(b, s // tm),
        in_specs=[row(d), _moe_spec(tm, d // LANES, s), pl.BlockSpec((1, 6, d), lambda i, j: (i, 0, 0)),
                  pl.BlockSpec(g.shape, lambda i, j: (0, 0))],
        out_specs=row(d),
        out_shape=jax.ShapeDtypeStruct((b, s, d), F32),
        compiler_params=_params(("arbitrary", "arbitrary")),
        name="final_norm",
    )(x, mo, mod, g)


def _rope_tables(positions):
    half = MLA_ROPE // 2
    inv = 1.0 / (ROPE_BASE ** (jnp.arange(0, MLA_ROPE, 2, dtype=F32) / MLA_ROPE))
    ang = positions.astype(F32)[..., None] * inv
    cos, sin = jnp.cos(ang), jnp.sin(ang)
    shp = positions.shape
    ones = jnp.ones(shp + (MLA_NOPE,), F32)
    z = lambda w: jnp.zeros(shp + (w,), F32)
    c = jnp.concatenate([ones, cos, cos, z(HEAD_PAD - MLA_NOPE - MLA_ROPE)], axis=-1)
    s1 = jnp.concatenate([z(MLA_NOPE), -sin, z(HEAD_PAD - MLA_NOPE - half)], axis=-1)
    s2 = jnp.concatenate([z(MLA_NOPE + half), sin, z(HEAD_PAD - MLA_NOPE - MLA_ROPE)], axis=-1)
    return c, s1, s2


def _pad_heads(w, per_head, keep):
    k = w.shape[0]
    w = w.reshape(k, MLA_HEADS, per_head)[:, :, :keep]
    w = jnp.pad(w, ((0, 0), (0, 0), (0, HEAD_PAD - keep)))
    return w.reshape(k, MLA_HEADS * HEAD_PAD)


def _router_weights(w_group, b_group, w_expert, b_expert):
    d = w_group.shape[0]
    pad = ROUTE_LANES - N_GROUPS - N_EXPERTS
    wr = jnp.concatenate([w_group, w_expert, jnp.zeros((d, pad), F32)], axis=1).astype(BF16)
    br = jnp.concatenate([b_group, b_expert, jnp.zeros((pad,), F32)])[None, :]
    return wr, br


def kernel(x, c, positions, adaln_w, adaln_b, norm_mix_g, norm_ffn_g, ab_w_in, mla_q_norm_g, mla_w_uq,
           mla_kv_norm_g, mla_w_ukv, conv_w, ab_w_out, cd_w_in, sg_v_norm_g, sg_w_s, sg_b_s, gla_w_g2, gla_b_g,
           gla_norm_g, cd_w_out, moe_w_group, moe_b_group, moe_w_expert, moe_b_expert, moe_w_gate_up, moe_w_down,
           final_norm_g):
    b, s, d = x.shape
    n = b * s
    tiles_per_seq = s // ROW_TILE
    mod = _adaln_mod(c, adaln_w, adaln_b)
    rc, rs1, rs2 = _rope_tables(positions)

    w = ab_w_in[0]
    q_w, kv_w, kr_w, b_w, c_w, x_w = jnp.split(w, [256, 384, 416, 928, 1440], axis=1)
    kr_w = jnp.pad(kr_w, ((0, 0), (MLA_NOPE, HEAD_PAD - MLA_NOPE - MLA_ROPE)))
    win0 = jnp.concatenate([q_w, kv_w, kr_w, b_w, c_w, x_w], axis=1).astype(BF16)
    wuq = _pad_heads(mla_w_uq[0], MLA_NOPE + MLA_ROPE, MLA_NOPE + MLA_ROPE).astype(BF16)
    wukv = mla_w_ukv[0].reshape(MLA_KV_RANK, MLA_HEADS, MLA_NOPE + MLA_V)
    wuk = _pad_heads(wukv[:, :, :MLA_NOPE].reshape(MLA_KV_RANK, -1), MLA_NOPE, MLA_NOPE).astype(BF16)
    wuv = wukv[:, :, MLA_NOPE:].reshape(MLA_KV_RANK, MLA_HEADS * MLA_V).astype(BF16)
    q, k, v, bconv = _l0_in(x, mod[0], norm_mix_g[0][None], win0, mla_q_norm_g[0][None], wuq,
                            mla_kv_norm_g[0][None], wuk, wuv, rc, rs1, rs2, conv_w[0])
    a = _attention(q, k, v)
    wout = ab_w_out[0].astype(BF16)
    wr, br = _router_weights(moe_w_group[0], moe_b_group[0], moe_w_expert[0], moe_b_expert[0])
    x1, h2, route0, cnt0 = _outproj_router(a.reshape(n, -1), bconv.reshape(n, -1), x.reshape(n, d), mod[0],
                                           wout[:512], wout[512:], norm_ffn_g[0][None], wr, br, tiles_per_seq)
    mo0 = _moe(h2, route0, cnt0, moe_w_gate_up[0].astype(BF16), moe_w_down[0].astype(BF16))

    w = cd_w_in[0]
    u_w, v_w, gq_w, gk_w, gv_w, gl_w, gr_w = jnp.split(w, [512, 1024, 1280, 1536, 2048, 2064], axis=1)
    gl_w = jnp.pad(gl_w, ((0, 0), (0, 128 - GLA_GATE_RANK)))
    win1 = jnp.concatenate([u_w, v_w, gq_w, gk_w, gv_w, gr_w, gl_w], axis=1).astype(BF16)
    wg2 = jnp.pad(gla_w_g2[0], ((0, 128 - GLA_GATE_RANK), (0, 0))).astype(BF16)
    x2, cout, gq, gk, gv, gr, la = _l1_in(
        x1.reshape(b, s, d), mo0, mod[0], mod[1], norm_mix_g[1][None], win1, sg_v_norm_g[0][None], sg_w_s[0], sg_b_s[0].T, wg2, gla_b_g[0][None])
    dout = _gla(gq, gk, gv, gr, la, gla_norm_g[0][None])
    wout = cd_w_out[0].astype(BF16)
    wr, br = _router_weights(moe_w_group[1], moe_b_group[1], moe_w_expert[1], moe_b_expert[1])
    x3, h2, route1, cnt1 = _outproj_router(cout.reshape(n, -1), dout.reshape(n, -1), x2.reshape(n, d), mod[1],
                                           wout[:512], wout[512:], norm_ffn_g[1][None], wr, br, tiles_per_seq)
    mo1 = _moe(h2, route1, cnt1, moe_w_gate_up[1].astype(BF16), moe_w_down[1].astype(BF16))
    return _final(x3.reshape(b, s, d), mo1, mod[1], final_norm_g[None])
```

```python
import functools

import jax
import jax.numpy as jnp
from jax import lax
from jax.experimental import pallas as pl
from jax.experimental.pallas import tpu as pltpu

F32 = jnp.float32
BF16 = jnp.bfloat16
HIGHEST = lax.Precision.HIGHEST

EPS = 1e-6
MLA_HEADS = 8
MLA_NOPE = 64
MLA_ROPE = 32
MLA_V = 64
MLA_Q_RANK = 256
MLA_KV_RANK = 128
ROPE_BASE = 10000.0
HEAD_PAD = 128
CONV_K = 3
SG_GROUPS = 4
SG_CHUNK = 128
GLA_HEADS = 4
GLA_DK = 64
GLA_DV = 128
GLA_GATE_RANK = 16
GLA_GATE_TAU = 16.0
GLA_CHUNK = 64
N_GROUPS = 4
EXPERTS_PER_GROUP = 8
N_EXPERTS = N_GROUPS * EXPERTS_PER_GROUP
D_EXPERT = 256
MOE_TILE = 2048
MOE_CHUNK = 128
MOE_EXPERTS_PER_STEP = 2
LANES = 128
ROUTE_LANES = 128
EXPERT_LANE0 = N_GROUPS

ROW_TILE = 512
ATTN_TK = 512
VMEM_LIMIT = 56 * 1024 * 1024
MOE_VMEM_LIMIT = 60 * 1024 * 1024


def _rms(x, g):
    return x * lax.rsqrt(jnp.mean(x * x, axis=-1, keepdims=True) + EPS) * g


def _silu(x):
    return x * jax.nn.sigmoid(x)


def _gelu_tanh(x):
    return 0.5 * x * (1.0 + jnp.tanh(0.7978845608028654 * (x + 0.044715 * (x * x * x))))


def _dot(a, b):
    return jnp.dot(a, b, preferred_element_type=F32)


def _params(sem):
    return pltpu.CompilerParams(dimension_semantics=sem, vmem_limit_bytes=VMEM_LIMIT)


def _to_lin(ref, val, lead=()):
    r, d = val.shape
    nb = d // LANES
    for jb in range(nb):
        ref[lead + (pl.ds(jb, r, stride=nb), slice(None))] = val[:, jb * LANES:(jb + 1) * LANES]


def _from_lin(ref, r, nb, lead=()):
    return jnp.concatenate([ref[lead + (pl.ds(jb, r, stride=nb), slice(None))] for jb in range(nb)], axis=1)


def _adaln_kernel(c_ref, w_ref, b_ref, o_ref):
    c = c_ref[...]
    o_ref[0] = jnp.dot(_silu(c), w_ref[0], preferred_element_type=F32, precision=HIGHEST) + b_ref[0]


def _adaln_mod(c, adaln_w, adaln_b):
    depth, d, d6 = adaln_w.shape
    b = c.shape[0]
    n_chunks = d6 // d
    out = pl.pallas_call(
        _adaln_kernel,
        grid=(depth, n_chunks),
        in_specs=[
            pl.BlockSpec((b, d), lambda l, j: (0, 0)),
            pl.BlockSpec((1, d, d), lambda l, j: (l, 0, j)),
            pl.BlockSpec((1, 1, d), lambda l, j: (l, 0, j)),
        ],
        out_specs=pl.BlockSpec((1, b, d), lambda l, j: (l, 0, j)),
        out_shape=jax.ShapeDtypeStruct((depth, b, d6), F32),
        compiler_params=_params(("arbitrary", "arbitrary")),
        name="adaln_mod",
    )(c, adaln_w, adaln_b.reshape(depth, 1, d6))
    return out.reshape(depth, b, n_chunks, d)


def _rope(blk, c, s1, s2):
    return blk * c + pltpu.roll(blk, HEAD_PAD - MLA_ROPE // 2, 1) * s1 + pltpu.roll(blk, MLA_ROPE // 2, 1) * s2


def _l0_in_kernel(x_ref, mod_ref, g_ref, win_ref, qg_ref, wuq_ref, kvg_ref, wuk_ref, wuv_ref,
                  rc_ref, rs1_ref, rs2_ref, cw_ref,
                  q_ref, k_ref, v_ref, bc_ref, ubuf):
    j = pl.program_id(1)
    tm = x_ref.shape[1]
    x = x_ref[0]
    m = mod_ref[0]
    h = _rms(x, g_ref[...]) * (1.0 + m[1:2]) + m[0:1]
    proj = _dot(h.astype(BF16), win_ref[...])
    q_lat = proj[:, 0:256]
    kv_lat = proj[:, 256:384]
    kr = proj[:, 384:512]
    bg = proj[:, 512:1024]
    cg = proj[:, 1024:1536]
    xi = proj[:, 1536:2048]

    c = rc_ref[0]
    s1 = rs1_ref[0]
    s2 = rs2_ref[0]
    scale = (MLA_NOPE + MLA_ROPE) ** -0.5 * 1.4426950408889634
    q = _dot(_rms(q_lat, qg_ref[...]).astype(BF16), wuq_ref[...])
    kvn = _rms(kv_lat, kvg_ref[...]).astype(BF16)
    kn = _dot(kvn, wuk_ref[...])
    vt = lax.dot_general(wuv_ref[...], kvn, (((1,), (1,)), ((), ())), preferred_element_type=F32)
    vrow = lax.broadcasted_iota(jnp.int32, vt.shape, 0)
    vt = jnp.where((vrow & (HEAD_PAD - 1)) == MLA_V, 1.0, vt).astype(BF16)
    for cb in range(tm // ATTN_TK):
        v_ref[0, cb] = vt[:, cb * ATTN_TK:(cb + 1) * ATTN_TK]
    krr = _rope(kr, c, s1, s2)
    for hd in range(MLA_HEADS):
        sl = slice(hd * HEAD_PAD, (hd + 1) * HEAD_PAD)
        q_ref[0, :, sl] = (_rope(q[:, sl], c, s1, s2) * scale).astype(BF16)
        k_ref[0, :, sl] = (kn[:, sl] + krr).astype(BF16)

    u = cg * xi

    @pl.when(j == 0)
    def _():
        ubuf[0:8, :] = jnp.zeros((8, ubuf.shape[1]), F32)

    ubuf[8:tm + 8, :] = u
    r1 = ubuf[7:tm + 7, :]
    r2 = ubuf[6:tm + 6, :]
    cw = cw_ref[...]
    y = cw[2:3] * u + cw[1:2] * r1 + cw[0:1] * r2
    bc_ref[0] = (bg * y).astype(BF16)
    ubuf[0:8, :] = u[tm - 8:tm, :]


def _l0_in(x, mod, g, win, qg, wuq, kvg, wuk, wuv, rc, rs1, rs2, cw):
    b, s, d = x.shape
    tm = ROW_TILE
    full = lambda a: pl.BlockSpec(a.shape, lambda i, j: (0,) * a.ndim)
    row = lambda w: pl.BlockSpec((1, tm, w), lambda i, j: (i, j, 0))
    return pl.pallas_call(
        _l0_in_kernel,
        grid=(b, s // tm),
        in_specs=[row(d), pl.BlockSpec((1, 6, d), lambda i, j: (i, 0, 0)), full(g), full(win), full(qg), full(wuq),
                  full(kvg), full(wuk), full(wuv), row(HEAD_PAD), row(HEAD_PAD), row(HEAD_PAD), full(cw)],
        out_specs=[row(MLA_HEADS * HEAD_PAD), row(MLA_HEADS * HEAD_PAD),
                   pl.BlockSpec((1, tm // ATTN_TK, MLA_HEADS * HEAD_PAD, ATTN_TK), lambda i, j: (i, j, 0, 0)),
                   row(d // 2)],
        out_shape=[jax.ShapeDtypeStruct((b, s, MLA_HEADS * HEAD_PAD), BF16),
                   jax.ShapeDtypeStruct((b, s, MLA_HEADS * HEAD_PAD), BF16),
                   jax.ShapeDtypeStruct((b, s // ATTN_TK, MLA_HEADS * HEAD_PAD, ATTN_TK), BF16),
                   jax.ShapeDtypeStruct((b, s, d // 2), BF16)],
        scratch_shapes=[pltpu.VMEM((tm + 8, d // 2), F32)],
        compiler_params=_params(("arbitrary", "arbitrary")),
        name="l0_in",
    )(x, mod, g, win, qg, wuq, kvg, wuk, wuv, rc, rs1, rs2, cw)


def _attn_kernel(q_ref, k_ref, v_ref, o_ref, s_buf, m_buf, acc_buf):
    qi = pl.program_id(2)
    tq = q_ref.shape[1]
    tk = ATTN_TK
    neg = jnp.finfo(F32).min
    heads = range(2)
    hsl = [slice(hd * HEAD_PAD, (hd + 1) * HEAD_PAD) for hd in heads]

    def scores(jj, hd):
        off = pl.multiple_of(jj * tk, tk)
        return lax.dot_general(k_ref[0, pl.ds(off, tk), hsl[hd]], q_ref[0, :, hsl[hd]], (((1,), (1,)), ((), ())),
                               preferred_element_type=F32)

    def consume(jj, s, hd, mask):
        if mask is not None:
            s = jnp.where(mask, s, neg)
        m = m_buf[hd]
        m_new = jnp.maximum(m, jnp.max(s, axis=0, keepdims=True))
        p = jnp.exp2(s - m_new).astype(BF16)
        pv = _dot(v_ref[0, jj, hsl[hd], :], p)
        acc_buf[hd] = jnp.exp2(m - m_new) * acc_buf[hd] + pv
        m_buf[hd] = m_new

    for hd in heads:
        m_buf[hd] = jnp.full((1, tq), neg, F32)
        acc_buf[hd] = jnp.zeros((HEAD_PAD, tq), F32)
        s_buf[hd] = scores(0, hd)

    def body(jj, c):
        nxt = [scores(jj + 1, hd) for hd in heads]
        for hd in heads:
            consume(jj, s_buf[hd], hd, None)
        for hd in heads:
            s_buf[hd] = nxt[hd]
        return c

    lax.fori_loop(0, qi, body, 0)
    krow = lax.broadcasted_iota(jnp.int32, (tk, tq), 0)
    qcol = lax.broadcasted_iota(jnp.int32, (tk, tq), 1)
    for hd in heads:
        consume(qi, s_buf[hd], hd, qcol >= krow)
    o = jnp.concatenate([acc_buf[hd][0:MLA_V] / acc_buf[hd][MLA_V:MLA_V + 1] for hd in heads], axis=0)
    o_ref[0] = o.T.astype(BF16)


def _attention(q, k, v):
    b, s, _ = q.shape
    tq = tk = ATTN_TK
    return pl.pallas_call(
        _attn_kernel,
        grid=(b, MLA_HEADS // 2, s // tq),
        in_specs=[pl.BlockSpec((1, tq, 2 * HEAD_PAD), lambda i, h, j: (i, j, h)),
                  pl.BlockSpec((1, s, 2 * HEAD_PAD), lambda i, h, j: (i, 0, h)),
                  pl.BlockSpec((1, s // tk, 2 * HEAD_PAD, tk), lambda i, h, j: (i, 0, h, 0))],
        out_specs=pl.BlockSpec((1, tq, 2 * MLA_V), lambda i, h, j: (i, j, h)),
        out_shape=jax.ShapeDtypeStruct((b, s, MLA_HEADS * MLA_V), BF16),
        scratch_shapes=[pltpu.VMEM((2, tk, tq), F32), pltpu.VMEM((2, 1, tq), F32),
                        pltpu.VMEM((2, HEAD_PAD, tq), F32)],
        compiler_params=_params(("arbitrary", "arbitrary", "arbitrary")),
        name="mla_attention",
    )(q, k, v)


def _outproj_router_kernel(a_ref, b_ref, x_ref, mod_ref, wa_ref, wb_ref, g_ref, wr_ref, br_ref,
                           x1_ref, h2_ref, route_ref, cnt_ref, carry):
    i = pl.program_id(0)
    tm = x_ref.shape[0]
    m = mod_ref[0]
    mix = _dot(a_ref[...], wa_ref[...]) + _dot(b_ref[...], wb_ref[...])
    x1 = x_ref[...] + m[2:3] * mix
    x1_ref[...] = x1
    h2 = _rms(x1, g_ref[...]) * (1.0 + m[4:5]) + m[3:4]
    _to_lin(h2_ref, h2)
    logits = _dot(h2.astype(BF16), wr_ref[...]) + br_ref[...]

    neg = jnp.finfo(F32).min
    lane = lax.broadcasted_iota(jnp.int32, (tm, ROUTE_LANES), 1).astype(F32)
    big = float(ROUTE_LANES)
    is_grp = lane < N_GROUPS
    lg = jnp.where(is_grp, logits, neg)
    gmax = jnp.max(lg, axis=-1, keepdims=True)
    gsel = jnp.min(jnp.where(lg == gmax, lane, big), axis=-1, keepdims=True)
    pg = 1.0 / jnp.sum(jnp.where(is_grp, jnp.exp(logits - gmax), 0.0), axis=-1, keepdims=True)
    lo = EXPERT_LANE0 + EXPERTS_PER_GROUP * gsel
    le = jnp.where((lane >= lo) & (lane < lo + EXPERTS_PER_GROUP), logits, neg)
    v0 = jnp.max(le, axis=-1, keepdims=True)
    l0 = jnp.min(jnp.where(le == v0, lane, big), axis=-1, keepdims=True)
    le2 = jnp.where(lane == l0, neg, le)
    v1 = jnp.max(le2, axis=-1, keepdims=True)
    l1 = jnp.min(jnp.where(le2 == v1, lane, big), axis=-1, keepdims=True)
    t = jnp.exp(v1 - v0)
    w0 = pg / (1.0 + t)
    w1 = pg * t / (1.0 + t)

    @pl.when(i % (MOE_TILE // tm) == 0)
    def _():
        carry[...] = jnp.zeros_like(carry)

    sel0 = lane == l0
    sel1 = lane == l1
    oh = jnp.where(sel0 | sel1, 1.0, 0.0)
    r = lax.broadcasted_iota(jnp.int32, (tm, tm), 0)
    cc = lax.broadcasted_iota(jnp.int32, (tm, tm), 1)
    stril = jnp.where(r > cc, 1.0, 0.0).astype(BF16)
    cum = _dot(stril, oh.astype(BF16)) + carry[...]
    rank0 = jnp.sum(jnp.where(sel0, cum, 0.0), axis=-1, keepdims=True)
    rank1 = jnp.sum(jnp.where(sel1, cum, 0.0), axis=-1, keepdims=True)
    carry[...] = carry[...] + jnp.sum(oh, axis=0, keepdims=True)
    cnt_ref[0] = carry[...]

    rec = jnp.where(lane == 0, l0 - EXPERT_LANE0, 0.0)
    rec = jnp.where(lane == 1, l1 - EXPERT_LANE0, rec)
    rec = jnp.where(lane == 2, rank0, rec)
    rec = jnp.where(lane == 3, rank1, rec)
    rec = jnp.where(lane == 4, w0, rec)
    rec = jnp.where(lane == 5, w1, rec)
    route_ref[...] = rec


def _outproj_router(a, bb, x, mod, wa, wb, g, wr, br, tiles_per_seq):
    n, d = x.shape
    tm = ROW_TILE
    nb = d // LANES
    full = lambda arr: pl.BlockSpec(arr.shape, lambda i: (0,) * arr.ndim)
    row = lambda w: pl.BlockSpec((tm, w), lambda i: (i, 0))
    return pl.pallas_call(
        _outproj_router_kernel,
        grid=(n // tm,),
        in_specs=[row(a.shape[1]), row(bb.shape[1]), row(d),
                  pl.BlockSpec((1, 6, d), lambda i: (i // tiles_per_seq, 0, 0)),
                  full(wa), full(wb), full(g), full(wr), full(br)],
        out_specs=[row(d), pl.BlockSpec((tm * nb, LANES), lambda i: (i, 0)), row(ROUTE_LANES),
                   pl.BlockSpec((1, 1, ROUTE_LANES), lambda i: (i // (MOE_TILE // tm), 0, 0))],
        out_shape=[jax.ShapeDtypeStruct((n, d), F32), jax.ShapeDtypeStruct((n * nb, LANES), F32),
                   jax.ShapeDtypeStruct((n, ROUTE_LANES), F32),
                   jax.ShapeDtypeStruct((n // MOE_TILE, 1, ROUTE_LANES), F32)],
        scratch_shapes=[pltpu.VMEM((1, ROUTE_LANES), F32)],
        compiler_params=_params(("arbitrary",)),
        name="outproj_router",
    )(a, bb, x, mod, wa, wb, g, wr, br)


def _moe_kernel(ls_ref, cn_ref, d_ref, wp_ref, h_ref, wgu_ref, wd_ref, o_ref, sbuf, ybuf):
    i = pl.program_id(0)
    ep = pl.program_id(1)
    nb = h_ref.shape[0] // MOE_TILE
    row_at = lambda r: pl.ds(pl.multiple_of(r * nb, nb), nb)

    @pl.when((i == 0) & (ep == 0))
    def _():
        sbuf[...] = jnp.zeros(sbuf.shape, F32)

    @pl.when(ep == 0)
    def _():
        def dispatch(g, c):
            for k in range(8):
                t = g * 8 + k
                row = h_ref[row_at(t), :]
                sbuf[row_at(d_ref[0, 0, 2 * t]), :] = row
                sbuf[row_at(d_ref[0, 0, 2 * t + 1]), :] = row
            return c

        lax.fori_loop(0, MOE_TILE // 8, dispatch, 0)

    for sub in range(MOE_EXPERTS_PER_STEP):
        e = ep * MOE_EXPERTS_PER_STEP + sub
        start = ls_ref[i * N_EXPERTS + e]
        cnt = cn_ref[i * N_EXPERTS + e]

        def chunk(c, carry, sub=sub, start=start, cnt=cnt):
            base = start + c * MOE_CHUNK
            x = jnp.concatenate(
                [sbuf[pl.ds(pl.multiple_of(base * nb, nb) + jb, MOE_CHUNK, stride=nb), :] for jb in range(nb)],
                axis=1)
            gu = _dot(x.astype(BF16), wgu_ref[sub])
            hmid = _silu(gu[:, :D_EXPERT]) * gu[:, D_EXPERT:]
            ybuf[...] = _dot(hmid.astype(BF16), wd_ref[sub])

            def group(g, cc):
                yg = ybuf[pl.ds(pl.multiple_of(g * 8, 8), 8), :]
                row0 = pl.multiple_of((base + g * 8) * nb, nb)
                for jb in range(nb):
                    sbuf[pl.ds(row0 + jb, 8, stride=nb), :] = yg[:, jb * LANES:(jb + 1) * LANES]
                return cc

            rows_left = jnp.minimum(cnt - c * MOE_CHUNK, MOE_CHUNK)
            lax.fori_loop(0, lax.shift_right_logical(rows_left + 7, 3), group, 0)
            return carry

        lax.fori_loop(0, lax.shift_right_logical(cnt + (MOE_CHUNK - 1), 7), chunk, 0)

    @pl.when(ep == N_EXPERTS // MOE_EXPERTS_PER_STEP - 1)
    def _():
        def combine(g, c):
            for k in range(8):
                t = g * 8 + k
                y0 = sbuf[row_at(d_ref[0, 0, 2 * t]), :]
                y1 = sbuf[row_at(d_ref[0, 0, 2 * t + 1]), :]
                o_ref[row_at(t), :] = wp_ref[0, 0, 2 * t] * y0 + wp_ref[0, 0, 2 * t + 1] * y1
            return c

        lax.fori_loop(0, MOE_TILE // 8, combine, 0)


def _moe(h2lin, route, cnt, wgu, wd):
    n_tiles = cnt.shape[0]
    d = wgu.shape[1]
    nb = d // LANES
    pairs = 2 * MOE_TILE
    e = route[:, 0:2].astype(jnp.int32).reshape(n_tiles, pairs)
    rank = route[:, 2:4].astype(jnp.int32).reshape(n_tiles, pairs)
    w = route[:, 4:6].reshape(n_tiles, 1, pairs)
    counts = cnt[:, 0, EXPERT_LANE0:EXPERT_LANE0 + N_EXPERTS].astype(jnp.int32)
    padded = (counts + 7) // 8 * 8
    lstart = jnp.cumsum(padded, axis=1) - padded
    seg = jnp.sum(jnp.where(e[:, :, None] == jnp.arange(N_EXPERTS, dtype=jnp.int32), lstart[:, None, :], 0), axis=2)
    dest = (seg + rank).reshape(n_tiles, 1, pairs)
    sorted_rows = pairs + 8 * N_EXPERTS + MOE_CHUNK

    eps = MOE_EXPERTS_PER_STEP
    smem = lambda: pl.BlockSpec((1, 1, pairs), lambda i, e, ls, cn: (i, 0, 0), memory_space=pltpu.SMEM)
    tile = lambda: pl.BlockSpec((MOE_TILE * nb, LANES), lambda i, e, ls, cn: (i, 0))
    grid_spec = pltpu.PrefetchScalarGridSpec(
        num_scalar_prefetch=2,
        grid=(n_tiles, N_EXPERTS // eps),
        in_specs=[smem(), smem(), tile(),
                  pl.BlockSpec((eps, d, 2 * D_EXPERT), lambda i, e, ls, cn: (e, 0, 0)),
                  pl.BlockSpec((eps, D_EXPERT, d), lambda i, e, ls, cn: (e, 0, 0))],
        out_specs=tile(),
        scratch_shapes=[pltpu.VMEM((sorted_rows * nb, LANES), F32), pltpu.VMEM((MOE_CHUNK, d), F32)],
    )
    return pl.pallas_call(
        _moe_kernel,
        grid_spec=grid_spec,
        out_shape=jax.ShapeDtypeStruct((n_tiles * MOE_TILE * nb, LANES), F32),
        compiler_params=pltpu.CompilerParams(dimension_semantics=("arbitrary", "arbitrary"),
                                             vmem_limit_bytes=MOE_VMEM_LIMIT),
        name="moe_ffn",
    )(lstart.reshape(-1), counts.reshape(-1), dest, w, h2lin, wgu, wd)


def _l1_in_kernel(x_ref, mo_ref, mod0_ref, mod1_ref, g_ref, win_ref, vg_ref, ws_ref, bst_ref,
                  wg2_ref, bg_ref,
                  x2_ref, c_ref, q_ref, k_ref, v_ref, r_ref, la_ref):
    tm = x_ref.shape[1]
    m1 = mod1_ref[0]
    x2 = x_ref[0] + mod0_ref[0][5:6] * _from_lin(mo_ref, tm, x_ref.shape[2] // LANES)
    x2_ref[0] = x2
    h = _rms(x2, g_ref[...]) * (1.0 + m1[1:2]) + m1[0:1]
    proj = _dot(h.astype(BF16), win_ref[...])
    u = _gelu_tanh(proj[:, 0:512])
    vn = _rms(_gelu_tanh(proj[:, 512:1024]), vg_ref[...]).astype(BF16)
    q_ref[0] = proj[:, 1024:1280].astype(BF16)
    k_ref[0] = proj[:, 1280:1536].astype(BF16)
    v_ref[0] = proj[:, 1536:2048].astype(BF16)
    r_ref[0] = proj[:, 2048:2560].astype(BF16)
    z = _dot(proj[:, 2560:2688].astype(BF16), wg2_ref[...]) + bg_ref[...]
    la_ref[0] = (jnp.minimum(z, 0.0) - jnp.log(1.0 + jnp.exp(-jnp.abs(z)))) * (1.0 / GLA_GATE_TAU)

    rr = lax.broadcasted_iota(jnp.int32, (SG_CHUNK, SG_CHUNK), 0)
    cc = lax.broadcasted_iota(jnp.int32, (SG_CHUNK, SG_CHUNK), 1)
    gd = 512 // SG_GROUPS
    for g in range(SG_GROUPS):
        wt = jnp.where(rr >= cc, ws_ref[g], 0.0).astype(BF16)
        bcol = bst_ref[:, g:g + 1]
        for ch in range(tm // SG_CHUNK):
            rows = slice(ch * SG_CHUNK, (ch + 1) * SG_CHUNK)
            cols = slice(g * gd, (g + 1) * gd)
            mixed = _dot(wt, vn[rows, cols]) + bcol
            c_ref[0, rows, cols] = (u[rows, cols] * mixed).astype(BF16)


def _l1_in(x, mo, mod0, mod1, g, win, vg, ws, bst, wg2, bg):
    b, s, d = x.shape
    tm = ROW_TILE
    full = lambda a: pl.BlockSpec(a.shape, lambda i, j: (0,) * a.ndim)
    row = lambda w: pl.BlockSpec((1, tm, w), lambda i, j: (i, j, 0))
    modspec = pl.BlockSpec((1, 6, d), lambda i, j: (i, 0, 0))
    widths = [d, 512, GLA_HEADS * GLA_DK, GLA_HEADS * GLA_DK, GLA_HEADS * GLA_DV, GLA_HEADS * GLA_DV,
              GLA_HEADS * GLA_DK]
    dtypes = [F32, BF16, BF16, BF16, BF16, BF16, F32]
    return pl.pallas_call(
        _l1_in_kernel,
        grid=(b, s // tm),
        in_specs=[row(d), pl.BlockSpec((tm * (d // LANES), LANES), lambda i, j: (i * (s // tm) + j, 0)), modspec, modspec, full(g), full(win), full(vg), full(ws),
                  full(bst), full(wg2), full(bg)],
        out_specs=[row(w) for w in widths],
        out_shape=[jax.ShapeDtypeStruct((b, s, w), dt) for w, dt in zip(widths, dtypes)],
        compiler_params=_params(("arbitrary", "arbitrary")),
        name="l1_in",
    )(x, mo, mod0, mod1, g, win, vg, ws, bst, wg2, bg)


def _gla_kernel(q_ref, k_ref, v_ref, r_ref, la_ref, gn_ref, o_ref, state):
    j = pl.program_id(1)
    tm = q_ref.shape[1]
    ck = GLA_CHUNK

    @pl.when(j == 0)
    def _():
        state[...] = jnp.zeros_like(state)

    rr = lax.broadcasted_iota(jnp.int32, (ck, ck), 0)
    cc = lax.broadcasted_iota(jnp.int32, (ck, ck), 1)
    causal = rr >= cc
    tril = jnp.where(causal, 1.0, 0.0)
    for ch in range(tm // ck):
        rows = slice(ch * ck, (ch + 1) * ck)
        la = la_ref[0, rows, :]
        bc = jnp.dot(tril, la, preferred_element_type=F32, precision=HIGHEST)
        b_last = bc[ck - 1:ck, :]
        q = q_ref[0, rows, :].astype(F32) * (GLA_DK ** -0.5)
        k = k_ref[0, rows, :].astype(F32)
        q_dec = (q * jnp.exp(bc)).astype(BF16)
        k_dec = (k * jnp.exp(-bc)).astype(BF16)
        k_end = k * jnp.exp(b_last - bc)
        decay = jnp.exp(b_last)
        for hd in range(GLA_HEADS):
            ks = slice(hd * GLA_DK, (hd + 1) * GLA_DK)
            vs = slice(hd * GLA_DV, (hd + 1) * GLA_DV)
            vh = v_ref[0, rows, vs]
            attn = lax.dot_general(q_dec[:, ks], k_dec[:, ks], (((1,), (1,)), ((), ())),
                                   preferred_element_type=F32)
            attn = jnp.where(causal, attn, 0.0)
            st = state[hd]
            o = _dot(attn.astype(BF16), vh) + lax.dot_general(
                q_dec[:, ks], st.astype(BF16), (((1,), (1,)), ((), ())), preferred_element_type=F32)
            kv = lax.dot_general(vh, k_end[:, ks].astype(BF16), (((0,), (0,)), ((), ())),
                                 preferred_element_type=F32)
            state[hd] = st * decay[:, ks] + kv
            o = _rms(o, gn_ref[:, vs])
            o_ref[0, rows, vs] = (_silu(r_ref[0, rows, vs].astype(F32)) * o).astype(BF16)


def _gla(q, k, v, r, la, gn):
    b, s, _ = q.shape
    tm = ROW_TILE
    row = lambda w: pl.BlockSpec((1, tm, w), lambda i, j: (i, j, 0))
    return pl.pallas_call(
        _gla_kernel,
        grid=(b, s // tm),
        in_specs=[row(q.shape[2]), row(k.shape[2]), row(v.shape[2]), row(r.shape[2]), row(la.shape[2]),
                  pl.BlockSpec(gn.shape, lambda i, j: (0, 0))],
        out_specs=row(v.shape[2]),
        out_shape=jax.ShapeDtypeStruct(v.shape, BF16),
        scratch_shapes=[pltpu.VMEM((GLA_HEADS, GLA_DV, GLA_DK), F32)],
        compiler_params=_params(("arbitrary", "arbitrary")),
        name="gla",
    )(q, k, v, r, la, gn)


def _final_kernel(x_ref, mo_ref, mod_ref, g_ref, o_ref):
    tm, d = x_ref.shape[1], x_ref.shape[2]
    x = x_ref[0] + mod_ref[0][5:6] * _from_lin(mo_ref, tm, d // LANES)
    o_ref[0] = _rms(x, g_ref[...])


def _final(x, mo, mod, g):
    b, s, d = x.shape
    tm = ROW_TILE
    row = lambda w: pl.BlockSpec((1, tm, w), lambda i, j: (i, j, 0))
    return pl.pallas_call(
        _final_kernel,
        grid=(b, s // tm),
        in_specs=[row(d), pl.BlockSpec((tm * (d // LANES), LANES), lambda i, j: (i * (s // tm) + j, 0)), pl.BlockSpec((1, 6, d), lambda i, j: (i, 0, 0)),
                  pl.BlockSpec(g.shape, lambda i, j: (0, 0))],
        out_specs=row(d),
        out_shape=jax.ShapeDtypeStruct((b, s, d), F32),
        compiler_params=_params(("arbitrary", "arbitrary")),
        name="final_norm",
    )(x, mo, mod, g)


def _rope_tables(positions):
    half = MLA_ROPE // 2
    inv = 1.0 / (ROPE_BASE ** (jnp.arange(0, MLA_ROPE, 2, dtype=F32) / MLA_ROPE))
    ang = positions.astype(F32)[..., None] * inv
    cos, sin = jnp.cos(ang), jnp.sin(ang)
    shp = positions.shape
    ones = jnp.ones(shp + (MLA_NOPE,), F32)
    z = lambda w: jnp.zeros(shp + (w,), F32)
    c = jnp.concatenate([ones, cos, cos, z(HEAD_PAD - MLA_NOPE - MLA_ROPE)], axis=-1)
    s1 = jnp.concatenate([z(MLA_NOPE), -sin, z(HEAD_PAD - MLA_NOPE - half)], axis=-1)
    s2 = jnp.concatenate([z(MLA_NOPE + half), sin, z(HEAD_PAD - MLA_NOPE - MLA_ROPE)], axis=-1)
    return c, s1, s2


def _pad_heads(w, per_head, keep):
    k = w.shape[0]
    w = w.reshape(k, MLA_HEADS, per_head)[:, :, :keep]
    w = jnp.pad(w, ((0, 0), (0, 0), (0, HEAD_PAD - keep)))
    return w.reshape(k, MLA_HEADS * HEAD_PAD)


def _router_weights(w_group, b_group, w_expert, b_expert):
    d = w_group.shape[0]
    pad = ROUTE_LANES - N_GROUPS - N_EXPERTS
    wr = jnp.concatenate([w_group, w_expert, jnp.zeros((d, pad), F32)], axis=1).astype(BF16)
    br = jnp.concatenate([b_group, b_expert, jnp.zeros((pad,), F32)])[None, :]
    return wr, br


def kernel(x, c, positions, adaln_w, adaln_b, norm_mix_g, norm_ffn_g, ab_w_in, mla_q_norm_g, mla_w_uq,
           mla_kv_norm_g, mla_w_ukv, conv_w, ab_w_out, cd_w_in, sg_v_norm_g, sg_w_s, sg_b_s, gla_w_g2, gla_b_g,
           gla_norm_g, cd_w_out, moe_w_group, moe_b_group, moe_w_expert, moe_b_expert, moe_w_gate_up, moe_w_down,
           final_norm_g):
    b, s, d = x.shape
    n = b * s
    tiles_per_seq = s // ROW_TILE
    mod = _adaln_mod(c, adaln_w, adaln_b)
    rc, rs1, rs2 = _rope_tables(positions)

    w = ab_w_in[0]
    q_w, kv_w, kr_w, b_w, c_w, x_w = jnp.split(w, [256, 384, 416, 928, 1440], axis=1)
    kr_w = jnp.pad(kr_w, ((0, 0), (MLA_NOPE, HEAD_PAD - MLA_NOPE - MLA_ROPE)))
    win0 = jnp.concatenate([q_w, kv_w, kr_w, b_w, c_w, x_w], axis=1).astype(BF16)
    wuq = _pad_heads(mla_w_uq[0], MLA_NOPE + MLA_ROPE, MLA_NOPE + MLA_ROPE).astype(BF16)
    wukv = mla_w_ukv[0].reshape(MLA_KV_RANK, MLA_HEADS, MLA_NOPE + MLA_V)
    wuk = _pad_heads(wukv[:, :, :MLA_NOPE].reshape(MLA_KV_RANK, -1), MLA_NOPE, MLA_NOPE).astype(BF16)
    wuv = _pad_heads(wukv[:, :, MLA_NOPE:].reshape(MLA_KV_RANK, -1), MLA_V, MLA_V).T.astype(BF16)
    q, k, v, bconv = _l0_in(x, mod[0], norm_mix_g[0][None], win0, mla_q_norm_g[0][None], wuq,
                            mla_kv_norm_g[0][None], wuk, wuv, rc, rs1, rs2, conv_w[0])
    a = _attention(q, k, v)
    wout = ab_w_out[0].astype(BF16)
    wr, br = _router_weights(moe_w_group[0], moe_b_group[0], moe_w_expert[0], moe_b_expert[0])
    x1, h2, route0, cnt0 = _outproj_router(a.reshape(n, -1), bconv.reshape(n, -1), x.reshape(n, d), mod[0],
                                           wout[:512], wout[512:], norm_ffn_g[0][None], wr, br, tiles_per_seq)
    mo0 = _moe(h2, route0, cnt0, moe_w_gate_up[0].astype(BF16), moe_w_down[0].astype(BF16))

    w = cd_w_in[0]
    u_w, v_w, gq_w, gk_w, gv_w, gl_w, gr_w = jnp.split(w, [512, 1024, 1280, 1536, 2048, 2064], axis=1)
    gl_w = jnp.pad(gl_w, ((0, 0), (0, 128 - GLA_GATE_RANK)))
    win1 = jnp.concatenate([u_w, v_w, gq_w, gk_w, gv_w, gr_w, gl_w], axis=1).astype(BF16)
    wg2 = jnp.pad(gla_w_g2[0], ((0, 128 - GLA_GATE_RANK), (0, 0))).astype(BF16)
    x2, cout, gq, gk, gv, gr, la = _l1_in(
        x1.reshape(b, s, d), mo0, mod[0], mod[1], norm_mix_g[1][None], win1, sg_v_norm_g[0][None], sg_w_s[0], sg_b_s[0].T, wg2, gla_b_g[0][None])
    dout = _gla(gq, gk, gv, gr, la, gla_norm_g[0][None])
    wout = cd_w_out[0].astype(BF16)
    wr, br = _router_weights(moe_w_group[1], moe_b_group[1], moe_w_expert[1], moe_b_expert[1])
    x3, h2, route1, cnt1 = _outproj_router(cout.reshape(n, -1), dout.reshape(n, -1), x2.reshape(n, d), mod[1],
                                           wout[:512], wout[512:], norm_ffn_g[1][None], wr, br, tiles_per_seq)
    mo1 = _moe(h2, route1, cnt1, moe_w_gate_up[1].astype(BF16), moe_w_down[1].astype(BF16))
    return _final(x3.reshape(b, s, d), mo1, mod[1], final_norm_g[None])
```

```python
import functools

import jax
import jax.numpy as jnp
from jax import lax
from jax.experimental import pallas as pl
from jax.experimental.pallas import tpu as pltpu

F32 = jnp.float32
BF16 = jnp.bfloat16
HIGHEST = lax.Precision.HIGHEST

EPS = 1e-6
MLA_HEADS = 8
MLA_NOPE = 64
MLA_ROPE = 32
MLA_V = 64
MLA_Q_RANK = 256
MLA_KV_RANK = 128
ROPE_BASE = 10000.0
HEAD_PAD = 128
CONV_K = 3
SG_GROUPS = 4
SG_CHUNK = 128
GLA_HEADS = 4
GLA_DK = 64
GLA_DV = 128
GLA_GATE_RANK = 16
GLA_GATE_TAU = 16.0
GLA_CHUNK = 64
GLA_BLOCK = 256
N_GROUPS = 4
EXPERTS_PER_GROUP = 8
N_EXPERTS = N_GROUPS * EXPERTS_PER_GROUP
D_EXPERT = 256
MOE_TILE = 2048
MOE_CHUNK = 160
ROUTE_REC = 8
MOE_EXPERTS_PER_STEP = 2
LANES = 128
ROUTE_LANES = 128
EXPERT_LANE0 = N_GROUPS

ROW_TILE = 512
ATTN_TK = 512
ATTN_HEADS = 4
VMEM_LIMIT = 56 * 1024 * 1024
MOE_VMEM_LIMIT = 60 * 1024 * 1024


def _rms(x, g):
    return x * lax.rsqrt(jnp.mean(x * x, axis=-1, keepdims=True) + EPS) * g


def _silu(x):
    return x * jax.nn.sigmoid(x)


def _gelu_tanh(x):
    return 0.5 * x * (1.0 + jnp.tanh(0.7978845608028654 * (x + 0.044715 * (x * x * x))))


def _dot(a, b):
    return jnp.dot(a, b, preferred_element_type=F32)


def _params(sem):
    return pltpu.CompilerParams(dimension_semantics=sem, vmem_limit_bytes=VMEM_LIMIT)


def _to_lin(ref, val, lead=()):
    r, d = val.shape
    nb = d // LANES
    for jb in range(nb):
        ref[lead + (pl.ds(jb, r, stride=nb), slice(None))] = val[:, jb * LANES:(jb + 1) * LANES]


def _from_lin(ref, r, nb, lead=()):
    return jnp.concatenate([ref[lead + (pl.ds(jb, r, stride=nb), slice(None))] for jb in range(nb)], axis=1)


def _adaln_kernel(c_ref, w_ref, b_ref, o_ref):
    c = c_ref[...]
    o_ref[0] = jnp.dot(_silu(c), w_ref[0], preferred_element_type=F32, precision=HIGHEST) + b_ref[0]


def _adaln_mod(c, adaln_w, adaln_b):
    depth, d, d6 = adaln_w.shape
    b = c.shape[0]
    n_chunks = d6 // d
    out = pl.pallas_call(
        _adaln_kernel,
        grid=(depth, n_chunks),
        in_specs=[
            pl.BlockSpec((b, d), lambda l, j: (0, 0)),
            pl.BlockSpec((1, d, d), lambda l, j: (l, 0, j)),
            pl.BlockSpec((1, 1, d), lambda l, j: (l, 0, j)),
        ],
        out_specs=pl.BlockSpec((1, b, d), lambda l, j: (l, 0, j)),
        out_shape=jax.ShapeDtypeStruct((depth, b, d6), F32),
        compiler_params=_params(("arbitrary", "arbitrary")),
        name="adaln_mod",
    )(c, adaln_w, adaln_b.reshape(depth, 1, d6))
    return out.reshape(depth, b, n_chunks, d)


def _rope(blk, c, s1, s2):
    return blk * c + pltpu.roll(blk, HEAD_PAD - MLA_ROPE // 2, 1) * s1 + pltpu.roll(blk, MLA_ROPE // 2, 1) * s2


def _l0_in_kernel(x_ref, mod_ref, g_ref, win_ref, qg_ref, wuq_ref, kvg_ref, wuk_ref, wuv_ref,
                  rt_ref, cw_ref,
                  q_ref, k_ref, v_ref, bc_ref, ubuf):
    j = pl.program_id(1)
    tm = x_ref.shape[1]
    x = x_ref[0]
    m = mod_ref[0]
    h = _rms(x, g_ref[...]) * (1.0 + m[1:2]) + m[0:1]
    proj = _dot(h.astype(BF16), win_ref[...])
    q_lat = proj[:, 0:256]
    kv_lat = proj[:, 256:384]
    kr = proj[:, 384:512]
    bg = proj[:, 512:1024]
    cg = proj[:, 1024:1536]
    xi = proj[:, 1536:2048]

    half = MLA_ROPE // 2
    tab = rt_ref[0]
    lane = lax.broadcasted_iota(jnp.int32, tab.shape, 1)
    in_x1 = (lane >= MLA_NOPE) & (lane < MLA_NOPE + half)
    in_x2 = (lane >= MLA_NOPE + half) & (lane < MLA_NOPE + MLA_ROPE)
    c = jnp.where(lane < MLA_NOPE, 1.0, jnp.where(in_x1, tab, jnp.where(in_x2, pltpu.roll(tab, half, 1), 0.0)))
    s1 = jnp.where(in_x1, -pltpu.roll(tab, HEAD_PAD - half, 1), 0.0)
    s2 = jnp.where(in_x2, tab, 0.0)
    scale = (MLA_NOPE + MLA_ROPE) ** -0.5 * 1.4426950408889634
    q = _dot(_rms(q_lat, qg_ref[...]).astype(BF16), wuq_ref[...])
    kvn = _rms(kv_lat, kvg_ref[...]).astype(BF16)
    kn = _dot(kvn, wuk_ref[...])
    vt = lax.dot_general(wuv_ref[...], kvn, (((1,), (1,)), ((), ())), preferred_element_type=F32)
    vrow = lax.broadcasted_iota(jnp.int32, vt.shape, 0)
    vt = jnp.where((vrow & (HEAD_PAD - 1)) == MLA_V, 1.0, vt).astype(BF16)
    for cb in range(tm // ATTN_TK):
        v_ref[0, cb] = vt[:, cb * ATTN_TK:(cb + 1) * ATTN_TK]
    krr = _rope(kr, c, s1, s2)
    for hd in range(MLA_HEADS):
        sl = slice(hd * HEAD_PAD, (hd + 1) * HEAD_PAD)
        q_ref[0, :, sl] = (_rope(q[:, sl], c, s1, s2) * scale).astype(BF16)
        k_ref[0, :, sl] = (kn[:, sl] + krr).astype(BF16)

    u = cg * xi

    @pl.when(j == 0)
    def _():
        ubuf[0:8, :] = jnp.zeros((8, ubuf.shape[1]), F32)

    ubuf[8:tm + 8, :] = u
    r1 = ubuf[7:tm + 7, :]
    r2 = ubuf[6:tm + 6, :]
    cw = cw_ref[...]
    y = cw[2:3] * u + cw[1:2] * r1 + cw[0:1] * r2
    bc_ref[0] = (bg * y).astype(BF16)
    ubuf[0:8, :] = u[tm - 8:tm, :]


def _l0_in(x, mod, g, win, qg, wuq, kvg, wuk, wuv, rt, cw):
    b, s, d = x.shape
    tm = ROW_TILE
    full = lambda a: pl.BlockSpec(a.shape, lambda i, j: (0,) * a.ndim)
    row = lambda w: pl.BlockSpec((1, tm, w), lambda i, j: (i, j, 0))
    return pl.pallas_call(
        _l0_in_kernel,
        grid=(b, s // tm),
        in_specs=[row(d), pl.BlockSpec((1, 6, d), lambda i, j: (i, 0, 0)), full(g), full(win), full(qg), full(wuq),
                  full(kvg), full(wuk), full(wuv), row(HEAD_PAD), full(cw)],
        out_specs=[row(MLA_HEADS * HEAD_PAD), row(MLA_HEADS * HEAD_PAD),
                   pl.BlockSpec((1, tm // ATTN_TK, MLA_HEADS * HEAD_PAD, ATTN_TK), lambda i, j: (i, j, 0, 0)),
                   row(d // 2)],
        out_shape=[jax.ShapeDtypeStruct((b, s, MLA_HEADS * HEAD_PAD), BF16),
                   jax.ShapeDtypeStruct((b, s, MLA_HEADS * HEAD_PAD), BF16),
                   jax.ShapeDtypeStruct((b, s // ATTN_TK, MLA_HEADS * HEAD_PAD, ATTN_TK), BF16),
                   jax.ShapeDtypeStruct((b, s, d // 2), BF16)],
        scratch_shapes=[pltpu.VMEM((tm + 8, d // 2), F32)],
        compiler_params=_params(("arbitrary", "arbitrary")),
        name="l0_in",
    )(x, mod, g, win, qg, wuq, kvg, wuk, wuv, rt, cw)


def _attn_kernel(q_ref, k_ref, v_ref, o_ref, s_buf, m_buf, acc_buf):
    qi = pl.program_id(2)
    tq = q_ref.shape[1]
    tk = ATTN_TK
    neg = jnp.finfo(F32).min
    heads = range(ATTN_HEADS)
    hsl = [slice(hd * HEAD_PAD, (hd + 1) * HEAD_PAD) for hd in heads]

    def scores(jj, hd):
        off = pl.multiple_of(jj * tk, tk)
        return lax.dot_general(k_ref[0, pl.ds(off, tk), hsl[hd]], q_ref[0, :, hsl[hd]], (((1,), (1,)), ((), ())),
                               preferred_element_type=F32)

    def consume(jj, s, hd, mask):
        if mask is not None:
            s = jnp.where(mask, s, neg)
        m = m_buf[hd]
        m_new = jnp.maximum(m, jnp.max(s, axis=0, keepdims=True))
        p = jnp.exp2(s - m_new).astype(BF16)
        pv = _dot(v_ref[0, jj, hsl[hd], :], p)
        acc_buf[hd] = jnp.exp2(m - m_new) * acc_buf[hd] + pv
        m_buf[hd] = m_new

    for hd in heads:
        m_buf[hd] = jnp.full((1, tq), neg, F32)
        acc_buf[hd] = jnp.zeros((HEAD_PAD, tq), F32)
        s_buf[hd] = scores(0, hd)

    def body(jj, c):
        nxt = [scores(jj + 1, hd) for hd in heads]
        for hd in heads:
            consume(jj, s_buf[hd], hd, None)
        for hd in heads:
            s_buf[hd] = nxt[hd]
        return c

    lax.fori_loop(0, qi, body, 0)
    krow = lax.broadcasted_iota(jnp.int32, (tk, tq), 0)
    qcol = lax.broadcasted_iota(jnp.int32, (tk, tq), 1)
    for hd in heads:
        consume(qi, s_buf[hd], hd, qcol >= krow)
    o = jnp.concatenate([acc_buf[hd][0:MLA_V] / acc_buf[hd][MLA_V:MLA_V + 1] for hd in heads], axis=0)
    o_ref[0] = o.T.astype(BF16)


def _attention(q, k, v):
    b, s, _ = q.shape
    tq = tk = ATTN_TK
    nh = ATTN_HEADS
    return pl.pallas_call(
        _attn_kernel,
        grid=(b, MLA_HEADS // nh, s // tq),
        in_specs=[pl.BlockSpec((1, tq, nh * HEAD_PAD), lambda i, h, j: (i, j, h)),
                  pl.BlockSpec((1, s, nh * HEAD_PAD), lambda i, h, j: (i, 0, h)),
                  pl.BlockSpec((1, s // tk, nh * HEAD_PAD, tk), lambda i, h, j: (i, 0, h, 0))],
        out_specs=pl.BlockSpec((1, tq, nh * MLA_V), lambda i, h, j: (i, j, h)),
        out_shape=jax.ShapeDtypeStruct((b, s, MLA_HEADS * MLA_V), BF16),
        scratch_shapes=[pltpu.VMEM((nh, tk, tq), F32), pltpu.VMEM((nh, 1, tq), F32),
                        pltpu.VMEM((nh, HEAD_PAD, tq), F32)],
        compiler_params=_params(("arbitrary", "arbitrary", "arbitrary")),
        name="mla_attention",
    )(q, k, v)


def _outproj_router_kernel(a_ref, b_ref, x_ref, mod_ref, wa_ref, wb_ref, g_ref, wr_ref, br_ref,
                           x1_ref, h2_ref, route_ref, cnt_ref, carry):
    i = pl.program_id(0)
    tm = x_ref.shape[0]
    m = mod_ref[0]
    mix = _dot(a_ref[...], wa_ref[...]) + _dot(b_ref[...], wb_ref[...])
    x1 = x_ref[...] + m[2:3] * mix
    x1_ref[...] = x1
    h2 = _rms(x1, g_ref[...]) * (1.0 + m[4:5]) + m[3:4]
    _to_lin(h2_ref, h2)
    logits = _dot(h2.astype(BF16), wr_ref[...]) + br_ref[...]

    neg = jnp.finfo(F32).min
    lane = lax.broadcasted_iota(jnp.int32, (tm, ROUTE_LANES), 1).astype(F32)
    big = float(ROUTE_LANES)
    is_grp = lane < N_GROUPS
    lg = jnp.where(is_grp, logits, neg)
    gmax = jnp.max(lg, axis=-1, keepdims=True)
    gsel = jnp.min(jnp.where(lg == gmax, lane, big), axis=-1, keepdims=True)
    pg = 1.0 / jnp.sum(jnp.where(is_grp, jnp.exp(logits - gmax), 0.0), axis=-1, keepdims=True)
    lo = EXPERT_LANE0 + EXPERTS_PER_GROUP * gsel
    le = jnp.where((lane >= lo) & (lane < lo + EXPERTS_PER_GROUP), logits, neg)
    v0 = jnp.max(le, axis=-1, keepdims=True)
    l0 = jnp.min(jnp.where(le == v0, lane, big), axis=-1, keepdims=True)
    le2 = jnp.where(lane == l0, neg, le)
    v1 = jnp.max(le2, axis=-1, keepdims=True)
    l1 = jnp.min(jnp.where(le2 == v1, lane, big), axis=-1, keepdims=True)
    t = jnp.exp(v1 - v0)
    w0 = pg / (1.0 + t)
    w1 = pg * t / (1.0 + t)

    @pl.when(i % (MOE_TILE // tm) == 0)
    def _():
        carry[...] = jnp.zeros_like(carry)

    sel0 = lane == l0
    sel1 = lane == l1
    oh = jnp.where(sel0 | sel1, 1.0, 0.0)
    r = lax.broadcasted_iota(jnp.int32, (tm, tm), 0)
    cc = lax.broadcasted_iota(jnp.int32, (tm, tm), 1)
    stril = jnp.where(r > cc, 1.0, 0.0).astype(BF16)
    cum = _dot(stril, oh.astype(BF16)) + carry[...]
    rank0 = jnp.sum(jnp.where(sel0, cum, 0.0), axis=-1, keepdims=True)
    rank1 = jnp.sum(jnp.where(sel1, cum, 0.0), axis=-1, keepdims=True)
    carry[...] = carry[...] + jnp.sum(oh, axis=0, keepdims=True)
    cnt_ref[0] = carry[...]

    rec = jnp.where(lane == 0, l0 - EXPERT_LANE0, 0.0)
    rec = jnp.where(lane == 1, l1 - EXPERT_LANE0, rec)
    rec = jnp.where(lane == 2, rank0, rec)
    rec = jnp.where(lane == 3, rank1, rec)
    rec = jnp.where(lane == 4, w0, rec)
    rec = jnp.where(lane == 5, w1, rec)
    route_ref[...] = rec[:, :ROUTE_REC]


def _outproj_router(a, bb, x, mod, wa, wb, g, wr, br, tiles_per_seq):
    n, d = x.shape
    tm = ROW_TILE
    nb = d // LANES
    full = lambda arr: pl.BlockSpec(arr.shape, lambda i: (0,) * arr.ndim)
    row = lambda w: pl.BlockSpec((tm, w), lambda i: (i, 0))
    return pl.pallas_call(
        _outproj_router_kernel,
        grid=(n // tm,),
        in_specs=[row(a.shape[1]), row(bb.shape[1]), row(d),
                  pl.BlockSpec((1, 6, d), lambda i: (i // tiles_per_seq, 0, 0)),
                  full(wa), full(wb), full(g), full(wr), full(br)],
        out_specs=[row(d), pl.BlockSpec((tm * nb, LANES), lambda i: (i, 0)), row(ROUTE_REC),
                   pl.BlockSpec((1, 1, ROUTE_LANES), lambda i: (i // (MOE_TILE // tm), 0, 0))],
        out_shape=[jax.ShapeDtypeStruct((n, d), F32), jax.ShapeDtypeStruct((n * nb, LANES), F32),
                   jax.ShapeDtypeStruct((n, ROUTE_REC), F32),
                   jax.ShapeDtypeStruct((n // MOE_TILE, 1, ROUTE_LANES), F32)],
        scratch_shapes=[pltpu.VMEM((1, ROUTE_LANES), F32)],
        compiler_params=_params(("arbitrary",)),
        name="outproj_router",
    )(a, bb, x, mod, wa, wb, g, wr, br)


def _moe_kernel(ls_ref, cn_ref, d_ref, wp_ref, h_ref, wgu_ref, wd_ref, o_ref, sbuf, ybuf):
    i = pl.program_id(0)
    ep = pl.program_id(1)
    nb = h_ref.shape[0] // MOE_TILE
    row_at = lambda r: pl.ds(pl.multiple_of(r * nb, nb), nb)

    @pl.when((i == 0) & (ep == 0))
    def _():
        sbuf[...] = jnp.zeros(sbuf.shape, F32)

    @pl.when(ep == 0)
    def _():
        def dispatch(g, c):
            for k in range(8):
                t = g * 8 + k
                row = h_ref[row_at(t), :]
                sbuf[row_at(d_ref[0, 0, 2 * t]), :] = row
                sbuf[row_at(d_ref[0, 0, 2 * t + 1]), :] = row
            return c

        lax.fori_loop(0, MOE_TILE // 8, dispatch, 0)

    for sub in range(MOE_EXPERTS_PER_STEP):
        e = ep * MOE_EXPERTS_PER_STEP + sub
        start = ls_ref[i * N_EXPERTS + e]
        cnt = cn_ref[i * N_EXPERTS + e]

        def chunk(c, carry, sub=sub, start=start, cnt=cnt):
            base = start + c * MOE_CHUNK
            x = jnp.concatenate(
                [sbuf[pl.ds(pl.multiple_of(base * nb, nb) + jb, MOE_CHUNK, stride=nb), :] for jb in range(nb)],
                axis=1)
            gu = _dot(x.astype(BF16), wgu_ref[sub])
            hmid = _silu(gu[:, :D_EXPERT]) * gu[:, D_EXPERT:]
            ybuf[...] = _dot(hmid.astype(BF16), wd_ref[sub])

            def group(g, cc):
                yg = ybuf[pl.ds(pl.multiple_of(g * 8, 8), 8), :]
                row0 = pl.multiple_of((base + g * 8) * nb, nb)
                for jb in range(nb):
                    sbuf[pl.ds(row0 + jb, 8, stride=nb), :] = yg[:, jb * LANES:(jb + 1) * LANES]
                return cc

            rows_left = jnp.minimum(cnt - c * MOE_CHUNK, MOE_CHUNK)
            lax.fori_loop(0, lax.shift_right_logical(rows_left + 7, 3), group, 0)
            return carry

        lax.fori_loop(0, (cnt + (MOE_CHUNK - 1)) // MOE_CHUNK, chunk, 0)

    @pl.when(ep == N_EXPERTS // MOE_EXPERTS_PER_STEP - 1)
    def _():
        def combine(g, c):
            for k in range(8):
                t = g * 8 + k
                y0 = sbuf[row_at(d_ref[0, 0, 2 * t]), :]
                y1 = sbuf[row_at(d_ref[0, 0, 2 * t + 1]), :]
                o_ref[row_at(t), :] = wp_ref[0, 0, 2 * t] * y0 + wp_ref[0, 0, 2 * t + 1] * y1
            return c

        lax.fori_loop(0, MOE_TILE // 8, combine, 0)


def _moe(h2lin, route, cnt, wgu, wd):
    n_tiles = cnt.shape[0]
    d = wgu.shape[1]
    nb = d // LANES
    pairs = 2 * MOE_TILE
    e = route[:, 0:2].astype(jnp.int32).reshape(n_tiles, pairs)
    rank = route[:, 2:4].astype(jnp.int32).reshape(n_tiles, pairs)
    w = route[:, 4:6].reshape(n_tiles, 1, pairs)
    counts = cnt[:, 0, EXPERT_LANE0:EXPERT_LANE0 + N_EXPERTS].astype(jnp.int32)
    padded = (counts + 7) // 8 * 8
    lstart = jnp.cumsum(padded, axis=1) - padded
    seg = jnp.sum(jnp.where(e[:, :, None] == jnp.arange(N_EXPERTS, dtype=jnp.int32), lstart[:, None, :], 0), axis=2)
    dest = (seg + rank).reshape(n_tiles, 1, pairs)
    sorted_rows = pairs + 8 * N_EXPERTS + MOE_CHUNK

    eps = MOE_EXPERTS_PER_STEP
    smem = lambda: pl.BlockSpec((1, 1, pairs), lambda i, e, ls, cn: (i, 0, 0), memory_space=pltpu.SMEM)
    tile = lambda: pl.BlockSpec((MOE_TILE * nb, LANES), lambda i, e, ls, cn: (i, 0))
    grid_spec = pltpu.PrefetchScalarGridSpec(
        num_scalar_prefetch=2,
        grid=(n_tiles, N_EXPERTS // eps),
        in_specs=[smem(), smem(), tile(),
                  pl.BlockSpec((eps, d, 2 * D_EXPERT), lambda i, e, ls, cn: (e, 0, 0)),
                  pl.BlockSpec((eps, D_EXPERT, d), lambda i, e, ls, cn: (e, 0, 0))],
        out_specs=tile(),
        scratch_shapes=[pltpu.VMEM((sorted_rows * nb, LANES), F32), pltpu.VMEM((MOE_CHUNK, d), F32)],
    )
    return pl.pallas_call(
        _moe_kernel,
        grid_spec=grid_spec,
        out_shape=jax.ShapeDtypeStruct((n_tiles * MOE_TILE * nb, LANES), F32),
        compiler_params=pltpu.CompilerParams(dimension_semantics=("arbitrary", "arbitrary"),
                                             vmem_limit_bytes=MOE_VMEM_LIMIT),
        name="moe_ffn",
    )(lstart.reshape(-1), counts.reshape(-1), dest, w, h2lin, wgu, wd)


def _l1_in_kernel(x_ref, mo_ref, mod0_ref, mod1_ref, g_ref, win_ref, vg_ref, ws_ref, bst_ref,
                  wg2_ref, bg_ref,
                  x2_ref, c_ref, q_ref, k_ref, v_ref, r_ref, la_ref):
    tm = x_ref.shape[1]
    m1 = mod1_ref[0]
    x2 = x_ref[0] + mod0_ref[0][5:6] * _from_lin(mo_ref, tm, x_ref.shape[2] // LANES)
    x2_ref[0] = x2
    h = _rms(x2, g_ref[...]) * (1.0 + m1[1:2]) + m1[0:1]
    proj = _dot(h.astype(BF16), win_ref[...])
    u = _gelu_tanh(proj[:, 0:512])
    vn = _rms(_gelu_tanh(proj[:, 512:1024]), vg_ref[...]).astype(BF16)
    q_ref[0] = proj[:, 1024:1280].astype(BF16)
    k_ref[0] = proj[:, 1280:1536].astype(BF16)
    v_ref[0] = proj[:, 1536:2048].astype(BF16)
    r_ref[0] = proj[:, 2048:2560].astype(BF16)
    z = _dot(proj[:, 2560:2688].astype(BF16), wg2_ref[...]) + bg_ref[...]
    la_ref[0] = (jnp.minimum(z, 0.0) - jnp.log(1.0 + jnp.exp(-jnp.abs(z)))) * (1.0 / GLA_GATE_TAU)

    rr = lax.broadcasted_iota(jnp.int32, (SG_CHUNK, SG_CHUNK), 0)
    cc = lax.broadcasted_iota(jnp.int32, (SG_CHUNK, SG_CHUNK), 1)
    gd = 512 // SG_GROUPS
    for g in range(SG_GROUPS):
        wt = jnp.where(rr >= cc, ws_ref[g], 0.0).astype(BF16)
        bcol = bst_ref[:, g:g + 1]
        for ch in range(tm // SG_CHUNK):
            rows = slice(ch * SG_CHUNK, (ch + 1) * SG_CHUNK)
            cols = slice(g * gd, (g + 1) * gd)
            mixed = _dot(wt, vn[rows, cols]) + bcol
            c_ref[0, rows, cols] = (u[rows, cols] * mixed).astype(BF16)


def _l1_in(x, mo, mod0, mod1, g, win, vg, ws, bst, wg2, bg):
    b, s, d = x.shape
    tm = ROW_TILE
    full = lambda a: pl.BlockSpec(a.shape, lambda i, j: (0,) * a.ndim)
    row = lambda w: pl.BlockSpec((1, tm, w), lambda i, j: (i, j, 0))
    modspec = pl.BlockSpec((1, 6, d), lambda i, j: (i, 0, 0))
    widths = [d, 512, GLA_HEADS * GLA_DK, GLA_HEADS * GLA_DK, GLA_HEADS * GLA_DV, GLA_HEADS * GLA_DV,
              GLA_HEADS * GLA_DK]
    dtypes = [F32, BF16, BF16, BF16, BF16, BF16, F32]
    return pl.pallas_call(
        _l1_in_kernel,
        grid=(b, s // tm),
        in_specs=[row(d), pl.BlockSpec((tm * (d // LANES), LANES), lambda i, j: (i * (s // tm) + j, 0)), modspec, modspec, full(g), full(win), full(vg), full(ws),
                  full(bst), full(wg2), full(bg)],
        out_specs=[row(w) for w in widths],
        out_shape=[jax.ShapeDtypeStruct((b, s, w), dt) for w, dt in zip(widths, dtypes)],
        compiler_params=_params(("arbitrary", "arbitrary")),
        name="l1_in",
    )(x, mo, mod0, mod1, g, win, vg, ws, bst, wg2, bg)


def _gla_kernel(q_ref, k_ref, v_ref, r_ref, la_ref, gn_ref, mask_ref, o_ref, state):
    j = pl.program_id(1)
    tm = q_ref.shape[1]
    ck = GLA_CHUNK
    n_ch = tm // ck
    nt = (((1,), (1,)), ((), ()))
    tn = (((0,), (0,)), ((), ()))

    @pl.when(j == 0)
    def _():
        state[...] = jnp.zeros_like(state)

    rr = lax.broadcasted_iota(jnp.int32, (ck, ck), 0)
    cc = lax.broadcasted_iota(jnp.int32, (ck, ck), 1)
    tril = jnp.where(rr >= cc, 1.0, 0.0)
    q = q_ref[0].astype(F32) * (GLA_DK ** -0.5)
    k = k_ref[0].astype(F32)
    q_dec, k_dec, k_end, b_last = [], [], [], []
    for ch in range(n_ch):
        rows = slice(ch * ck, (ch + 1) * ck)
        bc = jnp.dot(tril, la_ref[0, rows, :], preferred_element_type=F32, precision=HIGHEST)
        bl = bc[ck - 1:ck, :]
        b_last.append(bl)
        q_dec.append(q[rows] * jnp.exp(bc))
        k_dec.append(k[rows] * jnp.exp(-bc))
        k_end.append((k[rows] * jnp.exp(bl - bc)).astype(BF16))
    q_dec = jnp.concatenate(q_dec, axis=0)
    k_dec = jnp.concatenate(k_dec, axis=0).astype(BF16)
    decay_t = jnp.exp(jnp.concatenate(b_last, axis=0)).T
    lane = lax.broadcasted_iota(jnp.int32, (1, LANES), 1)
    mask = mask_ref[...] > 0.5
    srow = lax.broadcasted_iota(jnp.int32, (LANES, 2 * GLA_DV), 0)
    scol = lax.broadcasted_iota(jnp.int32, (LANES, 2 * GLA_DV), 1)
    diag = (srow < GLA_DK) == (scol < GLA_DV)
    for pair in range(GLA_HEADS // 2):
        ls = slice(pair * LANES, (pair + 1) * LANES)
        vs2 = slice(2 * pair * GLA_DV, (2 * pair + 2) * GLA_DV)
        qp = q_dec[:, ls].astype(BF16)
        st = state[pair]
        inter = []
        for ch in range(n_ch):
            rows = slice(ch * ck, (ch + 1) * ck)
            inter.append(_dot(qp[rows], st.astype(BF16)))
            kv = lax.dot_general(k_end[ch][:, ls], v_ref[0, rows, vs2], tn, preferred_element_type=F32)
            st = st * decay_t[ls, ch:ch + 1] + jnp.where(diag, kv, 0.0)
        state[pair] = st
        inter = jnp.concatenate(inter, axis=0)
        for sub in range(2):
            hd = 2 * pair + sub
            vs = slice(hd * GLA_DV, (hd + 1) * GLA_DV)
            mine = (lane >= sub * GLA_DK) & (lane < (sub + 1) * GLA_DK)
            qz = jnp.where(mine, q_dec[:, ls], 0.0).astype(BF16)
            intra = []
            for blk in range(tm // GLA_BLOCK):
                rows = slice(blk * GLA_BLOCK, (blk + 1) * GLA_BLOCK)
                attn = jnp.where(mask, lax.dot_general(qz[rows], k_dec[rows, ls], nt, preferred_element_type=F32), 0.0)
                intra.append(_dot(attn.astype(BF16), v_ref[0, rows, vs]))
            o = jnp.concatenate(intra, axis=0) + inter[:, sub * GLA_DV:(sub + 1) * GLA_DV]
            o = _rms(o, gn_ref[:, vs])
            o_ref[0, :, vs] = (_silu(r_ref[0, :, vs].astype(F32)) * o).astype(BF16)


def _gla(q, k, v, r, la, gn):
    b, s, _ = q.shape
    tm = ROW_TILE
    row = lambda w: pl.BlockSpec((1, tm, w), lambda i, j: (i, j, 0))
    pos = jnp.arange(GLA_BLOCK, dtype=jnp.int32)
    mask = ((pos[:, None] >= pos[None, :]) & (pos[:, None] // GLA_CHUNK == pos[None, :] // GLA_CHUNK)).astype(F32)
    return pl.pallas_call(
        _gla_kernel,
        grid=(b, s // tm),
        in_specs=[row(q.shape[2]), row(k.shape[2]), row(v.shape[2]), row(r.shape[2]), row(la.shape[2]),
                  pl.BlockSpec(gn.shape, lambda i, j: (0, 0)), pl.BlockSpec(mask.shape, lambda i, j: (0, 0))],
        out_specs=row(v.shape[2]),
        out_shape=jax.ShapeDtypeStruct(v.shape, BF16),
        scratch_shapes=[pltpu.VMEM((GLA_HEADS // 2, LANES, 2 * GLA_DV), F32)],
        compiler_params=_params(("arbitrary", "arbitrary")),
        name="gla",
    )(q, k, v, r, la, gn, mask)


def _final_kernel(x_ref, mo_ref, mod_ref, g_ref, o_ref):
    tm, d = x_ref.shape[1], x_ref.shape[2]
    x = x_ref[0] + mod_ref[0][5:6] * _from_lin(mo_ref, tm, d // LANES)
    o_ref[0] = _rms(x, g_ref[...])


def _final(x, mo, mod, g):
    b, s, d = x.shape
    tm = ROW_TILE
    row = lambda w: pl.BlockSpec((1, tm, w), lambda i, j: (i, j, 0))
    return pl.pallas_call(
        _final_kernel,
        grid=(b, s // tm),
        in_specs=[row(d), pl.BlockSpec((tm * (d // LANES), LANES), lambda i, j: (i * (s // tm) + j, 0)), pl.BlockSpec((1, 6, d), lambda i, j: (i, 0, 0)),
                  pl.BlockSpec(g.shape, lambda i, j: (0, 0))],
        out_specs=row(d),
        out_shape=jax.ShapeDtypeStruct((b, s, d), F32),
        compiler_params=_params(("arbitrary", "arbitrary")),
        name="final_norm",
    )(x, mo, mod, g)


def _rope_tables(positions):
    inv = 1.0 / (ROPE_BASE ** (jnp.arange(0, MLA_ROPE, 2, dtype=F32) / MLA_ROPE))
    ang = positions.astype(F32)[..., None] * inv
    z = lambda w: jnp.zeros(positions.shape + (w,), F32)
    return jnp.concatenate([z(MLA_NOPE), jnp.cos(ang), jnp.sin(ang), z(HEAD_PAD - MLA_NOPE - MLA_ROPE)], axis=-1)


def _pad_heads(w, per_head, keep):
    k = w.shape[0]
    w = w.reshape(k, MLA_HEADS, per_head)[:, :, :keep]
    w = jnp.pad(w, ((0, 0), (0, 0), (0, HEAD_PAD - keep)))
    return w.reshape(k, MLA_HEADS * HEAD_PAD)


def _router_weights(w_group, b_group, w_expert, b_expert):
    d = w_group.shape[0]
    pad = ROUTE_LANES - N_GROUPS - N_EXPERTS
    wr = jnp.concatenate([w_group, w_expert, jnp.zeros((d, pad), F32)], axis=1).astype(BF16)
    br = jnp.concatenate([b_group, b_expert, jnp.zeros((pad,), F32)])[None, :]
    return wr, br


def kernel(x, c, positions, adaln_w, adaln_b, norm_mix_g, norm_ffn_g, ab_w_in, mla_q_norm_g, mla_w_uq,
           mla_kv_norm_g, mla_w_ukv, conv_w, ab_w_out, cd_w_in, sg_v_norm_g, sg_w_s, sg_b_s, gla_w_g2, gla_b_g,
           gla_norm_g, cd_w_out, moe_w_group, moe_b_group, moe_w_expert, moe_b_expert, moe_w_gate_up, moe_w_down,
           final_norm_g):
    b, s, d = x.shape
    n = b * s
    tiles_per_seq = s // ROW_TILE
    mod = _adaln_mod(c, adaln_w, adaln_b)
    rt = _rope_tables(positions)

    w = ab_w_in[0]
    q_w, kv_w, kr_w, b_w, c_w, x_w = jnp.split(w, [256, 384, 416, 928, 1440], axis=1)
    kr_w = jnp.pad(kr_w, ((0, 0), (MLA_NOPE, HEAD_PAD - MLA_NOPE - MLA_ROPE)))
    win0 = jnp.concatenate([q_w, kv_w, kr_w, b_w, c_w, x_w], axis=1).astype(BF16)
    wuq = _pad_heads(mla_w_uq[0], MLA_NOPE + MLA_ROPE, MLA_NOPE + MLA_ROPE).astype(BF16)
    wukv = mla_w_ukv[0].reshape(MLA_KV_RANK, MLA_HEADS, MLA_NOPE + MLA_V)
    wuk = _pad_heads(wukv[:, :, :MLA_NOPE].reshape(MLA_KV_RANK, -1), MLA_NOPE, MLA_NOPE).astype(BF16)
    wuv = _pad_heads(wukv[:, :, MLA_NOPE:].reshape(MLA_KV_RANK, -1), MLA_V, MLA_V).T.astype(BF16)
    q, k, v, bconv = _l0_in(x, mod[0], norm_mix_g[0][None], win0, mla_q_norm_g[0][None], wuq,
                            mla_kv_norm_g[0][None], wuk, wuv, rt, conv_w[0])
    a = _attention(q, k, v)
    wout = ab_w_out[0].astype(BF16)
    wr, br = _router_weights(moe_w_group[0], moe_b_group[0], moe_w_expert[0], moe_b_expert[0])
    x1, h2, route0, cnt0 = _outproj_router(a.reshape(n, -1), bconv.reshape(n, -1), x.reshape(n, d), mod[0],
                                           wout[:512], wout[512:], norm_ffn_g[0][None], wr, br, tiles_per_seq)
    mo0 = _moe(h2, route0, cnt0, moe_w_gate_up[0].astype(BF16), moe_w_down[0].astype(BF16))

    w = cd_w_in[0]
    u_w, v_w, gq_w, gk_w, gv_w, gl_w, gr_w = jnp.split(w, [512, 1024, 1280, 1536, 2048, 2064], axis=1)
    gl_w = jnp.pad(gl_w, ((0, 0), (0, 128 - GLA_GATE_RANK)))
    win1 = jnp.concatenate([u_w, v_w, gq_w, gk_w, gv_w, gr_w, gl_w], axis=1).astype(BF16)
    wg2 = jnp.pad(gla_w_g2[0], ((0, 128 - GLA_GATE_RANK), (0, 0))).astype(BF16)
    x2, cout, gq, gk, gv, gr, la = _l1_in(
        x1.reshape(b, s, d), mo0, mod[0], mod[1], norm_mix_g[1][None], win1, sg_v_norm_g[0][None], sg_w_s[0], sg_b_s[0].T, wg2, gla_b_g[0][None])
    dout = _gla(gq, gk, gv, gr, la, gla_norm_g[0][None])
    wout = cd_w_out[0].astype(BF16)
    wr, br = _router_weights(moe_w_group[1], moe_b_group[1], moe_w_expert[1], moe_b_expert[1])
    x3, h2, route1, cnt1 = _outproj_router(cout.reshape(n, -1), dout.reshape(n, -1), x2.reshape(n, d), mod[1],
                                           wout[:512], wout[512:], norm_ffn_g[1][None], wr, br, tiles_per_seq)
    mo1 = _moe(h2, route1, cnt1, moe_w_gate_up[1].astype(BF16), moe_w_down[1].astype(BF16))
    return _final(x3.reshape(b, s, d), mo1, mod[1], final_norm_g[None])
```

```python
import functools

import jax
import jax.numpy as jnp
from jax import lax
from jax.experimental import pallas as pl
from jax.experimental.pallas import tpu as pltpu

F32 = jnp.float32
BF16 = jnp.bfloat16
HIGHEST = lax.Precision.HIGHEST

EPS = 1e-6
MLA_HEADS = 8
MLA_NOPE = 64
MLA_ROPE = 32
MLA_V = 64
MLA_Q_RANK = 256
MLA_KV_RANK = 128
ROPE_BASE = 10000.0
HEAD_PAD = 128
CONV_K = 3
SG_GROUPS = 4
SG_CHUNK = 128
GLA_HEADS = 4
GLA_DK = 64
GLA_DV = 128
GLA_GATE_RANK = 16
GLA_GATE_TAU = 16.0
GLA_CHUNK = 64
GLA_BLOCK = 256
N_GROUPS = 4
EXPERTS_PER_GROUP = 8
N_EXPERTS = N_GROUPS * EXPERTS_PER_GROUP
D_EXPERT = 256
MOE_TILE = 2048
MOE_CHUNK = 160
ROUTE_REC = 8
MOE_EXPERTS_PER_STEP = 2
LANES = 128
ROUTE_LANES = 128
EXPERT_LANE0 = N_GROUPS

ROW_TILE = 512
ATTN_TK = 512
ATTN_HEADS = 4
VMEM_LIMIT = 56 * 1024 * 1024
MOE_VMEM_LIMIT = 60 * 1024 * 1024


def _rms(x, g):
    return x * lax.rsqrt(jnp.mean(x * x, axis=-1, keepdims=True) + EPS) * g


def _silu(x):
    return x * jax.nn.sigmoid(x)


def _gelu_tanh(x):
    return 0.5 * x * (1.0 + jnp.tanh(0.7978845608028654 * (x + 0.044715 * (x * x * x))))


def _dot(a, b):
    return jnp.dot(a, b, preferred_element_type=F32)


def _params(sem):
    return pltpu.CompilerParams(dimension_semantics=sem, vmem_limit_bytes=VMEM_LIMIT)


def _to_lin(ref, val, lead=()):
    r, d = val.shape
    nb = d // LANES
    for jb in range(nb):
        ref[lead + (pl.ds(jb, r, stride=nb), slice(None))] = val[:, jb * LANES:(jb + 1) * LANES]


def _from_lin(ref, r, nb, lead=()):
    return jnp.concatenate([ref[lead + (pl.ds(jb, r, stride=nb), slice(None))] for jb in range(nb)], axis=1)


def _adaln_kernel(c_ref, w_ref, b_ref, o_ref):
    c = c_ref[...]
    o_ref[0] = jnp.dot(_silu(c), w_ref[0], preferred_element_type=F32, precision=HIGHEST) + b_ref[0]


def _adaln_mod(c, adaln_w, adaln_b):
    depth, d, d6 = adaln_w.shape
    b = c.shape[0]
    n_chunks = d6 // d
    out = pl.pallas_call(
        _adaln_kernel,
        grid=(depth, n_chunks),
        in_specs=[
            pl.BlockSpec((b, d), lambda l, j: (0, 0)),
            pl.BlockSpec((1, d, d), lambda l, j: (l, 0, j)),
            pl.BlockSpec((1, 1, d), lambda l, j: (l, 0, j)),
        ],
        out_specs=pl.BlockSpec((1, b, d), lambda l, j: (l, 0, j)),
        out_shape=jax.ShapeDtypeStruct((depth, b, d6), F32),
        compiler_params=_params(("arbitrary", "arbitrary")),
        name="adaln_mod",
    )(c, adaln_w, adaln_b.reshape(depth, 1, d6))
    return out.reshape(depth, b, n_chunks, d)


def _rope(blk, c, s1, s2):
    return blk * c + pltpu.roll(blk, HEAD_PAD - MLA_ROPE // 2, 1) * s1 + pltpu.roll(blk, MLA_ROPE // 2, 1) * s2


def _l0_in_kernel(x_ref, mod_ref, g_ref, win_ref, qg_ref, wuq_ref, kvg_ref, wuk_ref, wuv_ref,
                  rt_ref, cw_ref,
                  q_ref, k_ref, v_ref, bc_ref, ubuf):
    j = pl.program_id(1)
    tm = x_ref.shape[1]
    x = x_ref[0]
    m = mod_ref[0]
    h = _rms(x, g_ref[...]) * (1.0 + m[1:2]) + m[0:1]
    proj = _dot(h.astype(BF16), win_ref[...])
    q_lat = proj[:, 0:256]
    kv_lat = proj[:, 256:384]
    kr = proj[:, 384:512]
    bg = proj[:, 512:1024]
    cg = proj[:, 1024:1536]
    xi = proj[:, 1536:2048]

    half = MLA_ROPE // 2
    tab = rt_ref[0]
    lane = lax.broadcasted_iota(jnp.int32, tab.shape, 1)
    in_x1 = (lane >= MLA_NOPE) & (lane < MLA_NOPE + half)
    in_x2 = (lane >= MLA_NOPE + half) & (lane < MLA_NOPE + MLA_ROPE)
    c = jnp.where(lane < MLA_NOPE, 1.0, jnp.where(in_x1, tab, jnp.where(in_x2, pltpu.roll(tab, half, 1), 0.0)))
    s1 = jnp.where(in_x1, -pltpu.roll(tab, HEAD_PAD - half, 1), 0.0)
    s2 = jnp.where(in_x2, tab, 0.0)
    scale = (MLA_NOPE + MLA_ROPE) ** -0.5 * 1.4426950408889634
    q = _dot(_rms(q_lat, qg_ref[...]).astype(BF16), wuq_ref[...])
    kvn = _rms(kv_lat, kvg_ref[...]).astype(BF16)
    kn = _dot(kvn, wuk_ref[...])
    vt = lax.dot_general(wuv_ref[...], kvn, (((1,), (1,)), ((), ())), preferred_element_type=F32)
    vrow = lax.broadcasted_iota(jnp.int32, vt.shape, 0)
    vt = jnp.where((vrow & (HEAD_PAD - 1)) == MLA_V, 1.0, vt).astype(BF16)
    for cb in range(tm // ATTN_TK):
        v_ref[0, cb] = vt[:, cb * ATTN_TK:(cb + 1) * ATTN_TK]
    krr = _rope(kr, c, s1, s2)
    for hd in range(MLA_HEADS):
        sl = slice(hd * HEAD_PAD, (hd + 1) * HEAD_PAD)
        q_ref[0, :, sl] = (_rope(q[:, sl], c, s1, s2) * scale).astype(BF16)
        k_ref[0, :, sl] = (kn[:, sl] + krr).astype(BF16)

    u = cg * xi

    @pl.when(j == 0)
    def _():
        ubuf[0:8, :] = jnp.zeros((8, ubuf.shape[1]), F32)

    ubuf[8:tm + 8, :] = u
    r1 = ubuf[7:tm + 7, :]
    r2 = ubuf[6:tm + 6, :]
    cw = cw_ref[...]
    y = cw[2:3] * u + cw[1:2] * r1 + cw[0:1] * r2
    bc_ref[0] = (bg * y).astype(BF16)
    ubuf[0:8, :] = u[tm - 8:tm, :]


def _l0_in(x, mod, g, win, qg, wuq, kvg, wuk, wuv, rt, cw):
    b, s, d = x.shape
    tm = ROW_TILE
    full = lambda a: pl.BlockSpec(a.shape, lambda i, j: (0,) * a.ndim)
    row = lambda w: pl.BlockSpec((1, tm, w), lambda i, j: (i, j, 0))
    return pl.pallas_call(
        _l0_in_kernel,
        grid=(b, s // tm),
        in_specs=[row(d), pl.BlockSpec((1, 6, d), lambda i, j: (i, 0, 0)), full(g), full(win), full(qg), full(wuq),
                  full(kvg), full(wuk), full(wuv), row(HEAD_PAD), full(cw)],
        out_specs=[row(MLA_HEADS * HEAD_PAD), row(MLA_HEADS * HEAD_PAD),
                   pl.BlockSpec((1, tm // ATTN_TK, MLA_HEADS * HEAD_PAD, ATTN_TK), lambda i, j: (i, j, 0, 0)),
                   row(d // 2)],
        out_shape=[jax.ShapeDtypeStruct((b, s, MLA_HEADS * HEAD_PAD), BF16),
                   jax.ShapeDtypeStruct((b, s, MLA_HEADS * HEAD_PAD), BF16),
                   jax.ShapeDtypeStruct((b, s // ATTN_TK, MLA_HEADS * HEAD_PAD, ATTN_TK), BF16),
                   jax.ShapeDtypeStruct((b, s, d // 2), BF16)],
        scratch_shapes=[pltpu.VMEM((tm + 8, d // 2), F32)],
        compiler_params=_params(("arbitrary", "arbitrary")),
        name="l0_in",
    )(x, mod, g, win, qg, wuq, kvg, wuk, wuv, rt, cw)


def _attn_kernel(q_ref, k_ref, v_ref, o_ref, s_buf, m_buf, acc_buf):
    qi = pl.program_id(2)
    tq = q_ref.shape[1]
    tk = ATTN_TK
    neg = jnp.finfo(F32).min
    heads = range(ATTN_HEADS)
    hsl = [slice(hd * HEAD_PAD, (hd + 1) * HEAD_PAD) for hd in heads]

    def scores(jj, hd):
        off = pl.multiple_of(jj * tk, tk)
        return lax.dot_general(k_ref[0, pl.ds(off, tk), hsl[hd]], q_ref[0, :, hsl[hd]], (((1,), (1,)), ((), ())),
                               preferred_element_type=F32)

    def consume(jj, s, hd, mask):
        if mask is not None:
            s = jnp.where(mask, s, neg)
        m = m_buf[hd]
        m_new = jnp.maximum(m, jnp.max(s, axis=0, keepdims=True))
        p = jnp.exp2(s - m_new).astype(BF16)
        pv = _dot(v_ref[0, jj, hsl[hd], :], p)
        acc_buf[hd] = jnp.exp2(m - m_new) * acc_buf[hd] + pv
        m_buf[hd] = m_new

    for hd in heads:
        m_buf[hd] = jnp.full((1, tq), neg, F32)
        acc_buf[hd] = jnp.zeros((HEAD_PAD, tq), F32)
        s_buf[0, hd] = scores(0, hd)

    def advance(jj, cur):
        for hd in heads:
            s_buf[1 - cur, hd] = scores(jj + 1, hd)
        for hd in heads:
            consume(jj, s_buf[cur, hd], hd, None)

    def body(pp, c):
        advance(2 * pp, 0)
        advance(2 * pp + 1, 1)
        return c

    lax.fori_loop(0, qi // 2, body, 0)
    krow = lax.broadcasted_iota(jnp.int32, (tk, tq), 0)
    qcol = lax.broadcasted_iota(jnp.int32, (tk, tq), 1)
    odd = (qi & 1) == 1

    @pl.when(odd)
    def _():
        advance(qi - 1, 0)
        for hd in heads:
            consume(qi, s_buf[1, hd], hd, qcol >= krow)

    @pl.when(jnp.logical_not(odd))
    def _():
        for hd in heads:
            consume(qi, s_buf[0, hd], hd, qcol >= krow)
    o = jnp.concatenate([acc_buf[hd][0:MLA_V] / acc_buf[hd][MLA_V:MLA_V + 1] for hd in heads], axis=0)
    o_ref[0] = o.T.astype(BF16)


def _attention(q, k, v):
    b, s, _ = q.shape
    tq = tk = ATTN_TK
    nh = ATTN_HEADS
    return pl.pallas_call(
        _attn_kernel,
        grid=(b, MLA_HEADS // nh, s // tq),
        in_specs=[pl.BlockSpec((1, tq, nh * HEAD_PAD), lambda i, h, j: (i, j, h)),
                  pl.BlockSpec((1, s, nh * HEAD_PAD), lambda i, h, j: (i, 0, h)),
                  pl.BlockSpec((1, s // tk, nh * HEAD_PAD, tk), lambda i, h, j: (i, 0, h, 0))],
        out_specs=pl.BlockSpec((1, tq, nh * MLA_V), lambda i, h, j: (i, j, h)),
        out_shape=jax.ShapeDtypeStruct((b, s, MLA_HEADS * MLA_V), BF16),
        scratch_shapes=[pltpu.VMEM((2, nh, tk, tq), F32), pltpu.VMEM((nh, 1, tq), F32),
                        pltpu.VMEM((nh, HEAD_PAD, tq), F32)],
        compiler_params=_params(("arbitrary", "arbitrary", "arbitrary")),
        name="mla_attention",
    )(q, k, v)


def _outproj_router_kernel(a_ref, b_ref, x_ref, mod_ref, wa_ref, wb_ref, g_ref, wr_ref, br_ref,
                           x1_ref, h2_ref, route_ref, cnt_ref, carry):
    i = pl.program_id(0)
    tm = x_ref.shape[0]
    m = mod_ref[0]
    mix = _dot(a_ref[...], wa_ref[...]) + _dot(b_ref[...], wb_ref[...])
    x1 = x_ref[...] + m[2:3] * mix
    x1_ref[...] = x1
    h2 = _rms(x1, g_ref[...]) * (1.0 + m[4:5]) + m[3:4]
    _to_lin(h2_ref, h2)
    logits = _dot(h2.astype(BF16), wr_ref[...]) + br_ref[...]

    neg = jnp.finfo(F32).min
    lane = lax.broadcasted_iota(jnp.int32, (tm, ROUTE_LANES), 1).astype(F32)
    big = float(ROUTE_LANES)
    is_grp = lane < N_GROUPS
    lg = jnp.where(is_grp, logits, neg)
    gmax = jnp.max(lg, axis=-1, keepdims=True)
    gsel = jnp.min(jnp.where(lg == gmax, lane, big), axis=-1, keepdims=True)
    pg = 1.0 / jnp.sum(jnp.where(is_grp, jnp.exp(logits - gmax), 0.0), axis=-1, keepdims=True)
    lo = EXPERT_LANE0 + EXPERTS_PER_GROUP * gsel
    le = jnp.where((lane >= lo) & (lane < lo + EXPERTS_PER_GROUP), logits, neg)
    v0 = jnp.max(le, axis=-1, keepdims=True)
    l0 = jnp.min(jnp.where(le == v0, lane, big), axis=-1, keepdims=True)
    le2 = jnp.where(lane == l0, neg, le)
    v1 = jnp.max(le2, axis=-1, keepdims=True)
    l1 = jnp.min(jnp.where(le2 == v1, lane, big), axis=-1, keepdims=True)
    t = jnp.exp(v1 - v0)
    w0 = pg / (1.0 + t)
    w1 = pg * t / (1.0 + t)

    @pl.when(i % (MOE_TILE // tm) == 0)
    def _():
        carry[...] = jnp.zeros_like(carry)

    sel0 = lane == l0
    sel1 = lane == l1
    oh = jnp.where(sel0 | sel1, 1.0, 0.0)
    r = lax.broadcasted_iota(jnp.int32, (tm, tm), 0)
    cc = lax.broadcasted_iota(jnp.int32, (tm, tm), 1)
    stril = jnp.where(r > cc, 1.0, 0.0).astype(BF16)
    cum = _dot(stril, oh.astype(BF16)) + carry[...]
    rank0 = jnp.sum(jnp.where(sel0, cum, 0.0), axis=-1, keepdims=True)
    rank1 = jnp.sum(jnp.where(sel1, cum, 0.0), axis=-1, keepdims=True)
    carry[...] = carry[...] + jnp.sum(oh, axis=0, keepdims=True)
    cnt_ref[0] = carry[...]

    rec = jnp.where(lane == 0, l0 - EXPERT_LANE0, 0.0)
    rec = jnp.where(lane == 1, l1 - EXPERT_LANE0, rec)
    rec = jnp.where(lane == 2, rank0, rec)
    rec = jnp.where(lane == 3, rank1, rec)
    rec = jnp.where(lane == 4, w0, rec)
    rec = jnp.where(lane == 5, w1, rec)
    route_ref[...] = rec.T[:ROUTE_REC, :]


def _outproj_router(a, bb, x, mod, wa, wb, g, wr, br, tiles_per_seq):
    n, d = x.shape
    tm = ROW_TILE
    nb = d // LANES
    full = lambda arr: pl.BlockSpec(arr.shape, lambda i: (0,) * arr.ndim)
    row = lambda w: pl.BlockSpec((tm, w), lambda i: (i, 0))
    return pl.pallas_call(
        _outproj_router_kernel,
        grid=(n // tm,),
        in_specs=[row(a.shape[1]), row(bb.shape[1]), row(d),
                  pl.BlockSpec((1, 6, d), lambda i: (i // tiles_per_seq, 0, 0)),
                  full(wa), full(wb), full(g), full(wr), full(br)],
        out_specs=[row(d), pl.BlockSpec((tm * nb, LANES), lambda i: (i, 0)), pl.BlockSpec((ROUTE_REC, tm), lambda i: (0, i)),
                   pl.BlockSpec((1, 1, ROUTE_LANES), lambda i: (i // (MOE_TILE // tm), 0, 0))],
        out_shape=[jax.ShapeDtypeStruct((n, d), F32), jax.ShapeDtypeStruct((n * nb, LANES), F32),
                   jax.ShapeDtypeStruct((ROUTE_REC, n), F32),
                   jax.ShapeDtypeStruct((n // MOE_TILE, 1, ROUTE_LANES), F32)],
        scratch_shapes=[pltpu.VMEM((1, ROUTE_LANES), F32)],
        compiler_params=_params(("arbitrary",)),
        name="outproj_router",
    )(a, bb, x, mod, wa, wb, g, wr, br)


def _moe_kernel(ls_ref, cn_ref, d0_ref, d1_ref, w0_ref, w1_ref, h_ref, wgu_ref, wd_ref, o_ref, sbuf, ybuf):
    i = pl.program_id(0)
    ep = pl.program_id(1)
    nb = h_ref.shape[0] // MOE_TILE
    row_at = lambda r: pl.ds(pl.multiple_of(r * nb, nb), nb)

    @pl.when((i == 0) & (ep == 0))
    def _():
        sbuf[...] = jnp.zeros(sbuf.shape, F32)

    @pl.when(ep == 0)
    def _():
        def dispatch(g, c):
            for k in range(8):
                t = g * 8 + k
                row = h_ref[row_at(t), :]
                sbuf[row_at(d0_ref[0, 0, t]), :] = row
                sbuf[row_at(d1_ref[0, 0, t]), :] = row
            return c

        lax.fori_loop(0, MOE_TILE // 8, dispatch, 0)

    subs = range(MOE_EXPERTS_PER_STEP)
    starts = [ls_ref[i * N_EXPERTS + ep * MOE_EXPERTS_PER_STEP + sub] for sub in subs]
    cnts = [cn_ref[i * N_EXPERTS + ep * MOE_EXPERTS_PER_STEP + sub] for sub in subs]

    def ffn(base, sub):
        x = jnp.concatenate(
            [sbuf[pl.ds(pl.multiple_of(base * nb, nb) + jb, MOE_CHUNK, stride=nb), :] for jb in range(nb)], axis=1)
        gu = _dot(x.astype(BF16), wgu_ref[sub])
        hmid = _silu(gu[:, :D_EXPERT]) * gu[:, D_EXPERT:]
        return _dot(hmid.astype(BF16), wd_ref[sub])

    def write_back(base, rows, sub):
        def group(g, cc):
            yg = ybuf[sub, pl.ds(pl.multiple_of(g * 8, 8), 8), :]
            row0 = pl.multiple_of((base + g * 8) * nb, nb)
            for jb in range(nb):
                sbuf[pl.ds(row0 + jb, 8, stride=nb), :] = yg[:, jb * LANES:(jb + 1) * LANES]
            return cc

        def group4(g4, cc):
            for u in range(4):
                group(g4 * 4 + u, cc)
            return cc

        n8 = lax.shift_right_logical(jnp.clip(rows, 0, MOE_CHUNK) + 7, 3)
        n4 = lax.shift_right_logical(n8, 2)
        lax.fori_loop(0, n4, group4, 0)
        lax.fori_loop(n4 * 4, n8, group, 0)

    ys = [ffn(starts[sub], sub) for sub in subs]
    for sub in subs:
        ybuf[sub] = ys[sub]
    for sub in subs:
        write_back(starts[sub], cnts[sub], sub)

    for sub in subs:
        def chunk(c, carry, sub=sub):
            base = starts[sub] + c * MOE_CHUNK
            ybuf[sub] = ffn(base, sub)
            write_back(base, cnts[sub] - c * MOE_CHUNK, sub)
            return carry

        lax.fori_loop(1, (cnts[sub] + (MOE_CHUNK - 1)) // MOE_CHUNK, chunk, 0)

    @pl.when(ep == N_EXPERTS // MOE_EXPERTS_PER_STEP - 1)
    def _():
        def combine(g, c):
            for k in range(8):
                t = g * 8 + k
                y0 = sbuf[row_at(d0_ref[0, 0, t]), :]
                y1 = sbuf[row_at(d1_ref[0, 0, t]), :]
                o_ref[row_at(t), :] = w0_ref[0, 0, t] * y0 + w1_ref[0, 0, t] * y1
            return c

        lax.fori_loop(0, MOE_TILE // 8, combine, 0)


def _moe(h2lin, route, cnt, wgu, wd):
    n_tiles = cnt.shape[0]
    d = wgu.shape[1]
    nb = d // LANES
    pairs = 2 * MOE_TILE
    e = route[0:2].astype(jnp.int32).reshape(2, n_tiles, MOE_TILE)
    rank = route[2:4].astype(jnp.int32).reshape(2, n_tiles, MOE_TILE)
    w = route[4:6].reshape(2, n_tiles, 1, MOE_TILE)
    counts = cnt[:, 0, EXPERT_LANE0:EXPERT_LANE0 + N_EXPERTS].astype(jnp.int32)
    padded = (counts + 7) // 8 * 8
    lstart = jnp.cumsum(padded, axis=1) - padded
    experts = jnp.arange(N_EXPERTS, dtype=jnp.int32)[:, None, None, None]
    seg = jnp.sum(jnp.where(e[None] == experts, lstart.T[:, None, :, None], 0), axis=0)
    dest = (seg + rank)[:, :, None, :]
    sorted_rows = pairs + 8 * N_EXPERTS + MOE_CHUNK

    eps = MOE_EXPERTS_PER_STEP
    smem = lambda: pl.BlockSpec((1, 1, MOE_TILE), lambda i, e, ls, cn: (i, 0, 0), memory_space=pltpu.SMEM)
    tile = lambda: pl.BlockSpec((MOE_TILE * nb, LANES), lambda i, e, ls, cn: (i, 0))
    grid_spec = pltpu.PrefetchScalarGridSpec(
        num_scalar_prefetch=2,
        grid=(n_tiles, N_EXPERTS // eps),
        in_specs=[smem(), smem(), smem(), smem(), tile(),
                  pl.BlockSpec((eps, d, 2 * D_EXPERT), lambda i, e, ls, cn: (e, 0, 0)),
                  pl.BlockSpec((eps, D_EXPERT, d), lambda i, e, ls, cn: (e, 0, 0))],
        out_specs=tile(),
        scratch_shapes=[pltpu.VMEM((sorted_rows * nb, LANES), F32), pltpu.VMEM((eps, MOE_CHUNK, d), F32)],
    )
    return pl.pallas_call(
        _moe_kernel,
        grid_spec=grid_spec,
        out_shape=jax.ShapeDtypeStruct((n_tiles * MOE_TILE * nb, LANES), F32),
        compiler_params=pltpu.CompilerParams(dimension_semantics=("arbitrary", "arbitrary"),
                                             vmem_limit_bytes=MOE_VMEM_LIMIT),
        name="moe_ffn",
    )(lstart.reshape(-1), counts.reshape(-1), dest[0], dest[1], w[0], w[1], h2lin, wgu, wd)


def _l1_in_kernel(x_ref, mo_ref, mod0_ref, mod1_ref, g_ref, win_ref, vg_ref, ws_ref, bst_ref,
                  wg2_ref, bg_ref,
                  x2_ref, c_ref, q_ref, k_ref, v_ref, r_ref, la_ref):
    tm = x_ref.shape[1]
    m1 = mod1_ref[0]
    x2 = x_ref[0] + mod0_ref[0][5:6] * _from_lin(mo_ref, tm, x_ref.shape[2] // LANES)
    x2_ref[0] = x2
    h = _rms(x2, g_ref[...]) * (1.0 + m1[1:2]) + m1[0:1]
    proj = _dot(h.astype(BF16), win_ref[...])
    u = _gelu_tanh(proj[:, 0:512])
    vn = _rms(_gelu_tanh(proj[:, 512:1024]), vg_ref[...]).astype(BF16)
    q_ref[0] = proj[:, 1024:1280].astype(BF16)
    k_ref[0] = proj[:, 1280:1536].astype(BF16)
    v_ref[0] = proj[:, 1536:2048].astype(BF16)
    r_ref[0] = proj[:, 2048:2560].astype(BF16)
    z = _dot(proj[:, 2560:2688].astype(BF16), wg2_ref[...]) + bg_ref[...]
    la_ref[0] = (jnp.minimum(z, 0.0) - jnp.log(1.0 + jnp.exp(-jnp.abs(z)))) * (1.0 / GLA_GATE_TAU)

    rr = lax.broadcasted_iota(jnp.int32, (SG_CHUNK, SG_CHUNK), 0)
    cc = lax.broadcasted_iota(jnp.int32, (SG_CHUNK, SG_CHUNK), 1)
    gd = 512 // SG_GROUPS
    for g in range(SG_GROUPS):
        wt = jnp.where(rr >= cc, ws_ref[g], 0.0).astype(BF16)
        bcol = bst_ref[:, g:g + 1]
        for ch in range(tm // SG_CHUNK):
            rows = slice(ch * SG_CHUNK, (ch + 1) * SG_CHUNK)
            cols = slice(g * gd, (g + 1) * gd)
            mixed = _dot(wt, vn[rows, cols]) + bcol
            c_ref[0, rows, cols] = (u[rows, cols] * mixed).astype(BF16)


def _l1_in(x, mo, mod0, mod1, g, win, vg, ws, bst, wg2, bg):
    b, s, d = x.shape
    tm = ROW_TILE
    full = lambda a: pl.BlockSpec(a.shape, lambda i, j: (0,) * a.ndim)
    row = lambda w: pl.BlockSpec((1, tm, w), lambda i, j: (i, j, 0))
    modspec = pl.BlockSpec((1, 6, d), lambda i, j: (i, 0, 0))
    widths = [d, 512, GLA_HEADS * GLA_DK, GLA_HEADS * GLA_DK, GLA_HEADS * GLA_DV, GLA_HEADS * GLA_DV,
              GLA_HEADS * GLA_DK]
    dtypes = [F32, BF16, BF16, BF16, BF16, BF16, F32]
    return pl.pallas_call(
        _l1_in_kernel,
        grid=(b, s // tm),
        in_specs=[row(d), pl.BlockSpec((tm * (d // LANES), LANES), lambda i, j: (i * (s // tm) + j, 0)), modspec, modspec, full(g), full(win), full(vg), full(ws),
                  full(bst), full(wg2), full(bg)],
        out_specs=[row(w) for w in widths],
        out_shape=[jax.ShapeDtypeStruct((b, s, w), dt) for w, dt in zip(widths, dtypes)],
        compiler_params=_params(("arbitrary", "arbitrary")),
        name="l1_in",
    )(x, mo, mod0, mod1, g, win, vg, ws, bst, wg2, bg)


def _gla_kernel(q_ref, k_ref, v_ref, r_ref, la_ref, gn_ref, mask_ref, o_ref, state):
    j = pl.program_id(1)
    tm = q_ref.shape[1]
    ck = GLA_CHUNK
    n_ch = tm // ck
    nt = (((1,), (1,)), ((), ()))
    tn = (((0,), (0,)), ((), ()))

    @pl.when(j == 0)
    def _():
        state[...] = jnp.zeros_like(state)

    rr = lax.broadcasted_iota(jnp.int32, (ck, ck), 0)
    cc = lax.broadcasted_iota(jnp.int32, (ck, ck), 1)
    tril = jnp.where(rr >= cc, 1.0, 0.0)
    q = q_ref[0].astype(F32) * (GLA_DK ** -0.5)
    k = k_ref[0].astype(F32)
    q_dec, k_dec, k_end, b_last = [], [], [], []
    for ch in range(n_ch):
        rows = slice(ch * ck, (ch + 1) * ck)
        bc = jnp.dot(tril, la_ref[0, rows, :], preferred_element_type=F32, precision=HIGHEST)
        bl = bc[ck - 1:ck, :]
        b_last.append(bl)
        q_dec.append(q[rows] * jnp.exp(bc))
        k_dec.append(k[rows] * jnp.exp(-bc))
        k_end.append((k[rows] * jnp.exp(bl - bc)).astype(BF16))
    q_dec = jnp.concatenate(q_dec, axis=0)
    k_dec = jnp.concatenate(k_dec, axis=0).astype(BF16)
    decay_t = jnp.exp(jnp.concatenate(b_last, axis=0)).T
    lane = lax.broadcasted_iota(jnp.int32, (1, LANES), 1)
    mask = mask_ref[...] > 0.5
    srow = lax.broadcasted_iota(jnp.int32, (LANES, 2 * GLA_DV), 0)
    scol = lax.broadcasted_iota(jnp.int32, (LANES, 2 * GLA_DV), 1)
    diag = (srow < GLA_DK) == (scol < GLA_DV)
    for pair in range(GLA_HEADS // 2):
        ls = slice(pair * LANES, (pair + 1) * LANES)
        vs2 = slice(2 * pair * GLA_DV, (2 * pair + 2) * GLA_DV)
        qp = q_dec[:, ls].astype(BF16)
        st = state[pair]
        inter = []
        for ch in range(n_ch):
            rows = slice(ch * ck, (ch + 1) * ck)
            inter.append(_dot(qp[rows], st.astype(BF16)))
            kv = lax.dot_general(k_end[ch][:, ls], v_ref[0, rows, vs2], tn, preferred_element_type=F32)
            st = st * decay_t[ls, ch:ch + 1] + jnp.where(diag, kv, 0.0)
        state[pair] = st
        inter = jnp.concatenate(inter, axis=0)
        for sub in range(2):
            hd = 2 * pair + sub
            vs = slice(hd * GLA_DV, (hd + 1) * GLA_DV)
            mine = (lane >= sub * GLA_DK) & (lane < (sub + 1) * GLA_DK)
            qz = jnp.where(mine, q_dec[:, ls], 0.0).astype(BF16)
            intra = []
            for blk in range(tm // GLA_BLOCK):
                rows = slice(blk * GLA_BLOCK, (blk + 1) * GLA_BLOCK)
                attn = jnp.where(mask, lax.dot_general(qz[rows], k_dec[rows, ls], nt, preferred_element_type=F32), 0.0)
                intra.append(_dot(attn.astype(BF16), v_ref[0, rows, vs]))
            o = jnp.concatenate(intra, axis=0) + inter[:, sub * GLA_DV:(sub + 1) * GLA_DV]
            o = _rms(o, gn_ref[:, vs])
            o_ref[0, :, vs] = (_silu(r_ref[0, :, vs].astype(F32)) * o).astype(BF16)


def _gla(q, k, v, r, la, gn):
    b, s, _ = q.shape
    tm = ROW_TILE
    row = lambda w: pl.BlockSpec((1, tm, w), lambda i, j: (i, j, 0))
    pos = jnp.arange(GLA_BLOCK, dtype=jnp.int32)
    mask = ((pos[:, None] >= pos[None, :]) & (pos[:, None] // GLA_CHUNK == pos[None, :] // GLA_CHUNK)).astype(F32)
    return pl.pallas_call(
        _gla_kernel,
        grid=(b, s // tm),
        in_specs=[row(q.shape[2]), row(k.shape[2]), row(v.shape[2]), row(r.shape[2]), row(la.shape[2]),
                  pl.BlockSpec(gn.shape, lambda i, j: (0, 0)), pl.BlockSpec(mask.shape, lambda i, j: (0, 0))],
        out_specs=row(v.shape[2]),
        out_shape=jax.ShapeDtypeStruct(v.shape, BF16),
        scratch_shapes=[pltpu.VMEM((GLA_HEADS // 2, LANES, 2 * GLA_DV), F32)],
        compiler_params=_params(("arbitrary", "arbitrary")),
        name="gla",
    )(q, k, v, r, la, gn, mask)


def _final_kernel(x_ref, mo_ref, mod_ref, g_ref, o_ref):
    tm, d = x_ref.shape[1], x_ref.shape[2]
    x = x_ref[0] + mod_ref[0][5:6] * _from_lin(mo_ref, tm, d // LANES)
    o_ref[0] = _rms(x, g_ref[...])


def _final(x, mo, mod, g):
    b, s, d = x.shape
    tm = ROW_TILE
    row = lambda w: pl.BlockSpec((1, tm, w), lambda i, j: (i, j, 0))
    return pl.pallas_call(
        _final_kernel,
        grid=(b, s // tm),
        in_specs=[row(d), pl.BlockSpec((tm * (d // LANES), LANES), lambda i, j: (i * (s // tm) + j, 0)), pl.BlockSpec((1, 6, d), lambda i, j: (i, 0, 0)),
                  pl.BlockSpec(g.shape, lambda i, j: (0, 0))],
        out_specs=row(d),
        out_shape=jax.ShapeDtypeStruct((b, s, d), F32),
        compiler_params=_params(("arbitrary", "arbitrary")),
        name="final_norm",
    )(x, mo, mod, g)


def _rope_tables(positions):
    inv = 1.0 / (ROPE_BASE ** (jnp.arange(0, MLA_ROPE, 2, dtype=F32) / MLA_ROPE))
    ang = positions.astype(F32)[..., None] * inv
    z = lambda w: jnp.zeros(positions.shape + (w,), F32)
    cos, sin = lax.optimization_barrier((jnp.cos(ang), jnp.sin(ang)))
    return jnp.concatenate([z(MLA_NOPE), cos, sin, z(HEAD_PAD - MLA_NOPE - MLA_ROPE)], axis=-1)


def _pad_heads(w, per_head, keep):
    k = w.shape[0]
    w = w.reshape(k, MLA_HEADS, per_head)[:, :, :keep]
    w = jnp.pad(w, ((0, 0), (0, 0), (0, HEAD_PAD - keep)))
    return w.reshape(k, MLA_HEADS * HEAD_PAD)


def _router_weights(w_group, b_group, w_expert, b_expert):
    d = w_group.shape[0]
    pad = ROUTE_LANES - N_GROUPS - N_EXPERTS
    wr = jnp.concatenate([w_group, w_expert, jnp.zeros((d, pad), F32)], axis=1).astype(BF16)
    br = jnp.concatenate([b_group, b_expert, jnp.zeros((pad,), F32)])[None, :]
    return wr, br


def kernel(x, c, positions, adaln_w, adaln_b, norm_mix_g, norm_ffn_g, ab_w_in, mla_q_norm_g, mla_w_uq,
           mla_kv_norm_g, mla_w_ukv, conv_w, ab_w_out, cd_w_in, sg_v_norm_g, sg_w_s, sg_b_s, gla_w_g2, gla_b_g,
           gla_norm_g, cd_w_out, moe_w_group, moe_b_group, moe_w_expert, moe_b_expert, moe_w_gate_up, moe_w_down,
           final_norm_g):
    b, s, d = x.shape
    n = b * s
    tiles_per_seq = s // ROW_TILE
    mod = _adaln_mod(c, adaln_w, adaln_b)
    rt = _rope_tables(positions)

    w = ab_w_in[0]
    q_w, kv_w, kr_w, b_w, c_w, x_w = jnp.split(w, [256, 384, 416, 928, 1440], axis=1)
    kr_w = jnp.pad(kr_w, ((0, 0), (MLA_NOPE, HEAD_PAD - MLA_NOPE - MLA_ROPE)))
    win0 = jnp.concatenate([q_w, kv_w, kr_w, b_w, c_w, x_w], axis=1).astype(BF16)
    wuq = _pad_heads(mla_w_uq[0], MLA_NOPE + MLA_ROPE, MLA_NOPE + MLA_ROPE).astype(BF16)
    wukv = mla_w_ukv[0].reshape(MLA_KV_RANK, MLA_HEADS, MLA_NOPE + MLA_V)
    wuk = _pad_heads(wukv[:, :, :MLA_NOPE].reshape(MLA_KV_RANK, -1), MLA_NOPE, MLA_NOPE).astype(BF16)
    wuv = _pad_heads(wukv[:, :, MLA_NOPE:].reshape(MLA_KV_RANK, -1), MLA_V, MLA_V).T.astype(BF16)
    q, k, v, bconv = _l0_in(x, mod[0], norm_mix_g[0][None], win0, mla_q_norm_g[0][None], wuq,
                            mla_kv_norm_g[0][None], wuk, wuv, rt, conv_w[0])
    a = _attention(q, k, v)
    wout = ab_w_out[0].astype(BF16)
    wr, br = _router_weights(moe_w_group[0], moe_b_group[0], moe_w_expert[0], moe_b_expert[0])
    x1, h2, route0, cnt0 = _outproj_router(a.reshape(n, -1), bconv.reshape(n, -1), x.reshape(n, d), mod[0],
                                           wout[:512], wout[512:], norm_ffn_g[0][None], wr, br, tiles_per_seq)
    mo0 = _moe(h2, route0, cnt0, moe_w_gate_up[0].astype(BF16), moe_w_down[0].astype(BF16))

    w = cd_w_in[0]
    u_w, v_w, gq_w, gk_w, gv_w, gl_w, gr_w = jnp.split(w, [512, 1024, 1280, 1536, 2048, 2064], axis=1)
    gl_w = jnp.pad(gl_w, ((0, 0), (0, 128 - GLA_GATE_RANK)))
    win1 = jnp.concatenate([u_w, v_w, gq_w, gk_w, gv_w, gr_w, gl_w], axis=1).astype(BF16)
    wg2 = jnp.pad(gla_w_g2[0], ((0, 128 - GLA_GATE_RANK), (0, 0))).astype(BF16)
    x2, cout, gq, gk, gv, gr, la = _l1_in(
        x1.reshape(b, s, d), mo0, mod[0], mod[1], norm_mix_g[1][None], win1, sg_v_norm_g[0][None], sg_w_s[0], sg_b_s[0].T, wg2, gla_b_g[0][None])
    dout = _gla(gq, gk, gv, gr, la, gla_norm_g[0][None])
    wout = cd_w_out[0].astype(BF16)
    wr, br = _router_weights(moe_w_group[1], moe_b_group[1], moe_w_expert[1], moe_b_expert[1])
    x3, h2, route1, cnt1 = _outproj_router(cout.reshape(n, -1), dout.reshape(n, -1), x2.reshape(n, d), mod[1],
                                           wout[:512], wout[512:], norm_ffn_g[1][None], wr, br, tiles_per_seq)
    mo1 = _moe(h2, route1, cnt1, moe_w_gate_up[1].astype(BF16), moe_w_down[1].astype(BF16))
    return _final(x3.reshape(b, s, d), mo1, mod[1], final_norm_g[None])
```

```python
import functools

import jax
import jax.numpy as jnp
from jax import lax
from jax.experimental import pallas as pl
from jax.experimental.pallas import tpu as pltpu

F32 = jnp.float32
BF16 = jnp.bfloat16
HIGHEST = lax.Precision.HIGHEST

EPS = 1e-6
MLA_HEADS = 8
MLA_NOPE = 64
MLA_ROPE = 32
MLA_V = 64
MLA_Q_RANK = 256
MLA_KV_RANK = 128
ROPE_BASE = 10000.0
HEAD_PAD = 128
CONV_K = 3
SG_GROUPS = 4
SG_CHUNK = 128
GLA_HEADS = 4
GLA_DK = 64
GLA_DV = 128
GLA_GATE_RANK = 16
GLA_GATE_TAU = 16.0
GLA_CHUNK = 64
GLA_BLOCK = 256
N_GROUPS = 4
EXPERTS_PER_GROUP = 8
N_EXPERTS = N_GROUPS * EXPERTS_PER_GROUP
D_EXPERT = 256
MOE_TILE = 2048
MOE_CHUNK = 160
ROUTE_REC = 8
MOE_EXPERTS_PER_STEP = 2
LANES = 128
ROUTE_LANES = 128
EXPERT_LANE0 = N_GROUPS

ROW_TILE = 512
ATTN_TK = 512
ATTN_HEADS = 4
VMEM_LIMIT = 56 * 1024 * 1024
MOE_VMEM_LIMIT = 60 * 1024 * 1024


def _rms(x, g):
    return x * lax.rsqrt(jnp.mean(x * x, axis=-1, keepdims=True) + EPS) * g


def _silu(x):
    return x * jax.nn.sigmoid(x)


def _gelu_tanh(x):
    return 0.5 * x * (1.0 + jnp.tanh(0.7978845608028654 * (x + 0.044715 * (x * x * x))))


def _dot(a, b):
    return jnp.dot(a, b, preferred_element_type=F32)


def _params(sem):
    return pltpu.CompilerParams(dimension_semantics=sem, vmem_limit_bytes=VMEM_LIMIT)


def _to_lin(ref, val, lead=()):
    r, d = val.shape
    nb = d // LANES
    for jb in range(nb):
        ref[lead + (pl.ds(jb, r, stride=nb), slice(None))] = val[:, jb * LANES:(jb + 1) * LANES]


def _from_lin(ref, r, nb, lead=()):
    return jnp.concatenate([ref[lead + (pl.ds(jb, r, stride=nb), slice(None))] for jb in range(nb)], axis=1)


def _adaln_kernel(c_ref, w_ref, b_ref, o_ref):
    c = c_ref[...]
    o_ref[0] = jnp.dot(_silu(c), w_ref[0], preferred_element_type=F32, precision=HIGHEST) + b_ref[0]


def _adaln_mod(c, adaln_w, adaln_b):
    depth, d, d6 = adaln_w.shape
    b = c.shape[0]
    n_chunks = d6 // d
    out = pl.pallas_call(
        _adaln_kernel,
        grid=(depth, n_chunks),
        in_specs=[
            pl.BlockSpec((b, d), lambda l, j: (0, 0)),
            pl.BlockSpec((1, d, d), lambda l, j: (l, 0, j)),
            pl.BlockSpec((1, 1, d), lambda l, j: (l, 0, j)),
        ],
        out_specs=pl.BlockSpec((1, b, d), lambda l, j: (l, 0, j)),
        out_shape=jax.ShapeDtypeStruct((depth, b, d6), F32),
        compiler_params=_params(("arbitrary", "arbitrary")),
        name="adaln_mod",
    )(c, adaln_w, adaln_b.reshape(depth, 1, d6))
    return out.reshape(depth, b, n_chunks, d)


def _rope(blk, c, s1, s2):
    return blk * c + pltpu.roll(blk, HEAD_PAD - MLA_ROPE // 2, 1) * s1 + pltpu.roll(blk, MLA_ROPE // 2, 1) * s2


def _l0_in_kernel(x_ref, mod_ref, g_ref, win_ref, qg_ref, wuq_ref, kvg_ref, wuk_ref, wuv_ref,
                  rt_ref, cw_ref,
                  q_ref, k_ref, v_ref, bc_ref, ubuf):
    j = pl.program_id(1)
    tm = x_ref.shape[1]
    x = x_ref[0]
    m = mod_ref[0]
    h = _rms(x, g_ref[...]) * (1.0 + m[1:2]) + m[0:1]
    proj = _dot(h.astype(BF16), win_ref[...])
    q_lat = proj[:, 0:256]
    kv_lat = proj[:, 256:384]
    kr = proj[:, 384:512]
    bg = proj[:, 512:1024]
    cg = proj[:, 1024:1536]
    xi = proj[:, 1536:2048]

    half = MLA_ROPE // 2
    tab = rt_ref[0]
    lane = lax.broadcasted_iota(jnp.int32, tab.shape, 1)
    in_x1 = (lane >= MLA_NOPE) & (lane < MLA_NOPE + half)
    in_x2 = (lane >= MLA_NOPE + half) & (lane < MLA_NOPE + MLA_ROPE)
    c = jnp.where(lane < MLA_NOPE, 1.0, jnp.where(in_x1, tab, jnp.where(in_x2, pltpu.roll(tab, half, 1), 0.0)))
    s1 = jnp.where(in_x1, -pltpu.roll(tab, HEAD_PAD - half, 1), 0.0)
    s2 = jnp.where(in_x2, tab, 0.0)
    scale = (MLA_NOPE + MLA_ROPE) ** -0.5 * 1.4426950408889634
    q = _dot(_rms(q_lat, qg_ref[...]).astype(BF16), wuq_ref[...])
    kvn = _rms(kv_lat, kvg_ref[...]).astype(BF16)
    kn = _dot(kvn, wuk_ref[...])
    vt = lax.dot_general(wuv_ref[...], kvn, (((1,), (1,)), ((), ())), preferred_element_type=F32)
    vrow = lax.broadcasted_iota(jnp.int32, vt.shape, 0)
    vt = jnp.where((vrow & (HEAD_PAD - 1)) == MLA_V, 1.0, vt).astype(BF16)
    for cb in range(tm // ATTN_TK):
        v_ref[0, cb] = vt[:, cb * ATTN_TK:(cb + 1) * ATTN_TK]
    krr = _rope(kr, c, s1, s2)
    for hd in range(MLA_HEADS):
        sl = slice(hd * HEAD_PAD, (hd + 1) * HEAD_PAD)
        q_ref[0, :, sl] = (_rope(q[:, sl], c, s1, s2) * scale).astype(BF16)
        k_ref[0, :, sl] = (kn[:, sl] + krr).astype(BF16)

    u = cg * xi

    @pl.when(j == 0)
    def _():
        ubuf[0:8, :] = jnp.zeros((8, ubuf.shape[1]), F32)

    ubuf[8:tm + 8, :] = u
    r1 = ubuf[7:tm + 7, :]
    r2 = ubuf[6:tm + 6, :]
    cw = cw_ref[...]
    y = cw[2:3] * u + cw[1:2] * r1 + cw[0:1] * r2
    bc_ref[0] = (bg * y).astype(BF16)
    ubuf[0:8, :] = u[tm - 8:tm, :]


def _l0_in(x, mod, g, win, qg, wuq, kvg, wuk, wuv, rt, cw):
    b, s, d = x.shape
    tm = ROW_TILE
    full = lambda a: pl.BlockSpec(a.shape, lambda i, j: (0,) * a.ndim)
    row = lambda w: pl.BlockSpec((1, tm, w), lambda i, j: (i, j, 0))
    return pl.pallas_call(
        _l0_in_kernel,
        grid=(b, s // tm),
        in_specs=[row(d), pl.BlockSpec((1, 6, d), lambda i, j: (i, 0, 0)), full(g), full(win), full(qg), full(wuq),
                  full(kvg), full(wuk), full(wuv), row(HEAD_PAD), full(cw)],
        out_specs=[row(MLA_HEADS * HEAD_PAD), row(MLA_HEADS * HEAD_PAD),
                   pl.BlockSpec((1, tm // ATTN_TK, MLA_HEADS * HEAD_PAD, ATTN_TK), lambda i, j: (i, j, 0, 0)),
                   row(d // 2)],
        out_shape=[jax.ShapeDtypeStruct((b, s, MLA_HEADS * HEAD_PAD), BF16),
                   jax.ShapeDtypeStruct((b, s, MLA_HEADS * HEAD_PAD), BF16),
                   jax.ShapeDtypeStruct((b, s // ATTN_TK, MLA_HEADS * HEAD_PAD, ATTN_TK), BF16),
                   jax.ShapeDtypeStruct((b, s, d // 2), BF16)],
        scratch_shapes=[pltpu.VMEM((tm + 8, d // 2), F32)],
        compiler_params=_params(("arbitrary", "arbitrary")),
        name="l0_in",
    )(x, mod, g, win, qg, wuq, kvg, wuk, wuv, rt, cw)


def _attn_kernel(q_ref, k_ref, v_ref, o_ref, s_buf, m_buf, acc_buf):
    qi = pl.program_id(2)
    tq = q_ref.shape[1]
    tk = ATTN_TK
    neg = jnp.finfo(F32).min
    heads = range(ATTN_HEADS)
    hsl = [slice(hd * HEAD_PAD, (hd + 1) * HEAD_PAD) for hd in heads]

    def scores(jj, hd):
        off = pl.multiple_of(jj * tk, tk)
        return lax.dot_general(k_ref[0, pl.ds(off, tk), hsl[hd]], q_ref[0, :, hsl[hd]], (((1,), (1,)), ((), ())),
                               preferred_element_type=F32)

    def consume(jj, s, hd, mask):
        if mask is not None:
            s = jnp.where(mask, s, neg)
        m = m_buf[hd]
        m_new = jnp.maximum(m, jnp.max(s, axis=0, keepdims=True))
        p = jnp.exp2(s - m_new).astype(BF16)
        pv = _dot(v_ref[0, jj, hsl[hd], :], p)
        acc_buf[hd] = jnp.exp2(m - m_new) * acc_buf[hd] + pv
        m_buf[hd] = m_new

    for hd in heads:
        m_buf[hd] = jnp.full((1, tq), neg, F32)
        acc_buf[hd] = jnp.zeros((HEAD_PAD, tq), F32)
        s_buf[0, hd] = scores(0, hd)

    def advance(jj, cur):
        for hd in heads:
            s_buf[1 - cur, hd] = scores(jj + 1, hd)
        for hd in heads:
            consume(jj, s_buf[cur, hd], hd, None)

    def body(pp, c):
        advance(2 * pp, 0)
        advance(2 * pp + 1, 1)
        return c

    lax.fori_loop(0, qi // 2, body, 0)
    krow = lax.broadcasted_iota(jnp.int32, (tk, tq), 0)
    qcol = lax.broadcasted_iota(jnp.int32, (tk, tq), 1)
    odd = (qi & 1) == 1

    @pl.when(odd)
    def _():
        advance(qi - 1, 0)
        for hd in heads:
            consume(qi, s_buf[1, hd], hd, qcol >= krow)

    @pl.when(jnp.logical_not(odd))
    def _():
        for hd in heads:
            consume(qi, s_buf[0, hd], hd, qcol >= krow)
    o = jnp.concatenate([acc_buf[hd][0:MLA_V] / acc_buf[hd][MLA_V:MLA_V + 1] for hd in heads], axis=0)
    o_ref[0] = o.T.astype(BF16)


def _attention(q, k, v):
    b, s, _ = q.shape
    tq = tk = ATTN_TK
    nh = ATTN_HEADS
    return pl.pallas_call(
        _attn_kernel,
        grid=(b, MLA_HEADS // nh, s // tq),
        in_specs=[pl.BlockSpec((1, tq, nh * HEAD_PAD), lambda i, h, j: (i, j, h)),
                  pl.BlockSpec((1, s, nh * HEAD_PAD), lambda i, h, j: (i, 0, h)),
                  pl.BlockSpec((1, s // tk, nh * HEAD_PAD, tk), lambda i, h, j: (i, 0, h, 0))],
        out_specs=pl.BlockSpec((1, tq, nh * MLA_V), lambda i, h, j: (i, j, h)),
        out_shape=jax.ShapeDtypeStruct((b, s, MLA_HEADS * MLA_V), BF16),
        scratch_shapes=[pltpu.VMEM((2, nh, tk, tq), F32), pltpu.VMEM((nh, 1, tq), F32),
                        pltpu.VMEM((nh, HEAD_PAD, tq), F32)],
        compiler_params=_params(("arbitrary", "arbitrary", "arbitrary")),
        name="mla_attention",
    )(q, k, v)


def _outproj_router_kernel(a_ref, b_ref, x_ref, mod_ref, wa_ref, wb_ref, g_ref, wr_ref, br_ref, stril_ref,
                           x1_ref, h2_ref, route_ref, cnt_ref, carry):
    i = pl.program_id(0)
    tm = x_ref.shape[0]
    m = mod_ref[0]
    mix = _dot(a_ref[...], wa_ref[...]) + _dot(b_ref[...], wb_ref[...])
    x1 = x_ref[...] + m[2:3] * mix
    x1_ref[...] = x1
    h2 = _rms(x1, g_ref[...]) * (1.0 + m[4:5]) + m[3:4]
    _to_lin(h2_ref, h2)
    logits = _dot(h2.astype(BF16), wr_ref[...]) + br_ref[...]

    neg = jnp.finfo(F32).min
    lane = lax.broadcasted_iota(jnp.int32, (tm, ROUTE_LANES), 1).astype(F32)
    big = float(ROUTE_LANES)
    is_grp = lane < N_GROUPS
    lg = jnp.where(is_grp, logits, neg)
    gmax = jnp.max(lg, axis=-1, keepdims=True)
    gsel = jnp.min(jnp.where(lg == gmax, lane, big), axis=-1, keepdims=True)
    pg = 1.0 / jnp.sum(jnp.where(is_grp, jnp.exp(logits - gmax), 0.0), axis=-1, keepdims=True)
    lo = EXPERT_LANE0 + EXPERTS_PER_GROUP * gsel
    le = jnp.where((lane >= lo) & (lane < lo + EXPERTS_PER_GROUP), logits, neg)
    v0 = jnp.max(le, axis=-1, keepdims=True)
    l0 = jnp.min(jnp.where(le == v0, lane, big), axis=-1, keepdims=True)
    le2 = jnp.where(lane == l0, neg, le)
    v1 = jnp.max(le2, axis=-1, keepdims=True)
    l1 = jnp.min(jnp.where(le2 == v1, lane, big), axis=-1, keepdims=True)
    t = jnp.exp(v1 - v0)
    w0 = pg / (1.0 + t)
    w1 = pg * t / (1.0 + t)

    @pl.when(i % (MOE_TILE // tm) == 0)
    def _():
        carry[...] = jnp.zeros_like(carry)

    sel0 = lane == l0
    sel1 = lane == l1
    oh = jnp.where(sel0 | sel1, 1.0, 0.0)
    cum = _dot(stril_ref[...], oh.astype(BF16)) + carry[...]
    rank0 = jnp.sum(jnp.where(sel0, cum, 0.0), axis=-1, keepdims=True)
    rank1 = jnp.sum(jnp.where(sel1, cum, 0.0), axis=-1, keepdims=True)
    carry[...] = carry[...] + jnp.sum(oh, axis=0, keepdims=True)
    cnt_ref[0] = carry[...]

    rec = jnp.where(lane == 0, l0 - EXPERT_LANE0, 0.0)
    rec = jnp.where(lane == 1, l1 - EXPERT_LANE0, rec)
    rec = jnp.where(lane == 2, rank0, rec)
    rec = jnp.where(lane == 3, rank1, rec)
    rec = jnp.where(lane == 4, w0, rec)
    rec = jnp.where(lane == 5, w1, rec)
    route_ref[...] = rec.T[:ROUTE_REC, :]


def _outproj_router(a, bb, x, mod, wa, wb, g, wr, br, tiles_per_seq):
    n, d = x.shape
    tm = ROW_TILE
    nb = d // LANES
    full = lambda arr: pl.BlockSpec(arr.shape, lambda i: (0,) * arr.ndim)
    row = lambda w: pl.BlockSpec((tm, w), lambda i: (i, 0))
    pos = jnp.arange(tm, dtype=jnp.int32)
    stril = (pos[:, None] > pos[None, :]).astype(BF16)
    return pl.pallas_call(
        _outproj_router_kernel,
        grid=(n // tm,),
        in_specs=[row(a.shape[1]), row(bb.shape[1]), row(d),
                  pl.BlockSpec((1, 6, d), lambda i: (i // tiles_per_seq, 0, 0)),
                  full(wa), full(wb), full(g), full(wr), full(br), full(stril)],
        out_specs=[row(d), pl.BlockSpec((tm * nb, LANES), lambda i: (i, 0)), pl.BlockSpec((ROUTE_REC, tm), lambda i: (0, i)),
                   pl.BlockSpec((1, 1, ROUTE_LANES), lambda i: (i // (MOE_TILE // tm), 0, 0))],
        out_shape=[jax.ShapeDtypeStruct((n, d), F32), jax.ShapeDtypeStruct((n * nb, LANES), F32),
                   jax.ShapeDtypeStruct((ROUTE_REC, n), F32),
                   jax.ShapeDtypeStruct((n // MOE_TILE, 1, ROUTE_LANES), F32)],
        scratch_shapes=[pltpu.VMEM((1, ROUTE_LANES), F32)],
        compiler_params=_params(("arbitrary",)),
        name="outproj_router",
    )(a, bb, x, mod, wa, wb, g, wr, br, stril)


def _moe_kernel(ls_ref, cn_ref, d0_ref, d1_ref, w0_ref, w1_ref, h_ref, wgu_ref, wd_ref, o_ref, sbuf, ybuf):
    i = pl.program_id(0)
    ep = pl.program_id(1)
    nb = h_ref.shape[0] // MOE_TILE
    row_at = lambda r: pl.ds(pl.multiple_of(r * nb, nb), nb)
    lin_at = lambda a: pl.ds(pl.multiple_of(a, nb), nb)

    @pl.when((i == 0) & (ep == 0))
    def _():
        sbuf[...] = jnp.zeros(sbuf.shape, F32)

    @pl.when(ep == 0)
    def _():
        def dispatch(g, c):
            for k in range(8):
                t = g * 8 + k
                row = h_ref[row_at(t), :]
                sbuf[lin_at(d0_ref[0, 0, t]), :] = row
                sbuf[lin_at(d1_ref[0, 0, t]), :] = row
            return c

        lax.fori_loop(0, MOE_TILE // 8, dispatch, 0)

    subs = range(MOE_EXPERTS_PER_STEP)
    starts = [ls_ref[i * N_EXPERTS + ep * MOE_EXPERTS_PER_STEP + sub] for sub in subs]
    cnts = [cn_ref[i * N_EXPERTS + ep * MOE_EXPERTS_PER_STEP + sub] for sub in subs]

    def ffn(base, sub):
        x = jnp.concatenate(
            [sbuf[pl.ds(pl.multiple_of(base * nb, nb) + jb, MOE_CHUNK, stride=nb), :] for jb in range(nb)], axis=1)
        gu = _dot(x.astype(BF16), wgu_ref[sub])
        hmid = _silu(gu[:, :D_EXPERT]) * gu[:, D_EXPERT:]
        return _dot(hmid.astype(BF16), wd_ref[sub])

    def write_back(base, rows, sub):
        def group(g, cc):
            yg = ybuf[sub, pl.ds(pl.multiple_of(g * 8, 8), 8), :]
            row0 = pl.multiple_of((base + g * 8) * nb, nb)
            for jb in range(nb):
                sbuf[pl.ds(row0 + jb, 8, stride=nb), :] = yg[:, jb * LANES:(jb + 1) * LANES]
            return cc

        def group4(g4, cc):
            for u in range(4):
                group(g4 * 4 + u, cc)
            return cc

        n8 = lax.shift_right_logical(jnp.clip(rows, 0, MOE_CHUNK) + 7, 3)
        n4 = lax.shift_right_logical(n8, 2)
        lax.fori_loop(0, n4, group4, 0)
        lax.fori_loop(n4 * 4, n8, group, 0)

    ys = [ffn(starts[sub], sub) for sub in subs]
    for sub in subs:
        ybuf[sub] = ys[sub]
    for sub in subs:
        write_back(starts[sub], cnts[sub], sub)

    for sub in subs:
        def chunk(c, carry, sub=sub):
            base = starts[sub] + c * MOE_CHUNK
            ybuf[sub] = ffn(base, sub)
            write_back(base, cnts[sub] - c * MOE_CHUNK, sub)
            return carry

        lax.fori_loop(1, (cnts[sub] + (MOE_CHUNK - 1)) // MOE_CHUNK, chunk, 0)

    @pl.when(ep == N_EXPERTS // MOE_EXPERTS_PER_STEP - 1)
    def _():
        def combine(g, c):
            for k in range(8):
                t = g * 8 + k
                y0 = sbuf[lin_at(d0_ref[0, 0, t]), :]
                y1 = sbuf[lin_at(d1_ref[0, 0, t]), :]
                o_ref[row_at(t), :] = w0_ref[0, 0, t] * y0 + w1_ref[0, 0, t] * y1
            return c

        lax.fori_loop(0, MOE_TILE // 8, combine, 0)


def _moe(h2lin, route, cnt, wgu, wd):
    n_tiles = cnt.shape[0]
    d = wgu.shape[1]
    nb = d // LANES
    pairs = 2 * MOE_TILE
    e = route[0:2].astype(jnp.int32).reshape(2, n_tiles, MOE_TILE)
    rank = route[2:4].astype(jnp.int32).reshape(2, n_tiles, MOE_TILE)
    w = route[4:6].reshape(2, n_tiles, 1, MOE_TILE)
    counts = cnt[:, 0, EXPERT_LANE0:EXPERT_LANE0 + N_EXPERTS].astype(jnp.int32)
    padded = (counts + 7) // 8 * 8
    lstart = jnp.cumsum(padded, axis=1) - padded
    experts = jnp.arange(N_EXPERTS, dtype=jnp.int32)[:, None, None, None]
    seg = jnp.sum(jnp.where(e[None] == experts, lstart.T[:, None, :, None], 0), axis=0)
    dest = ((seg + rank) * nb)[:, :, None, :]
    sorted_rows = pairs + 8 * N_EXPERTS + MOE_CHUNK

    eps = MOE_EXPERTS_PER_STEP
    smem = lambda: pl.BlockSpec((1, 1, MOE_TILE), lambda i, e, ls, cn: (i, 0, 0), memory_space=pltpu.SMEM)
    tile = lambda: pl.BlockSpec((MOE_TILE * nb, LANES), lambda i, e, ls, cn: (i, 0))
    grid_spec = pltpu.PrefetchScalarGridSpec(
        num_scalar_prefetch=2,
        grid=(n_tiles, N_EXPERTS // eps),
        in_specs=[smem(), smem(), smem(), smem(), tile(),
                  pl.BlockSpec((eps, d, 2 * D_EXPERT), lambda i, e, ls, cn: (e, 0, 0)),
                  pl.BlockSpec((eps, D_EXPERT, d), lambda i, e, ls, cn: (e, 0, 0))],
        out_specs=tile(),
        scratch_shapes=[pltpu.VMEM((sorted_rows * nb, LANES), F32), pltpu.VMEM((eps, MOE_CHUNK, d), F32)],
    )
    return pl.pallas_call(
        _moe_kernel,
        grid_spec=grid_spec,
        out_shape=jax.ShapeDtypeStruct((n_tiles * MOE_TILE * nb, LANES), F32),
        compiler_params=pltpu.CompilerParams(dimension_semantics=("arbitrary", "arbitrary"),
                                             vmem_limit_bytes=MOE_VMEM_LIMIT),
        name="moe_ffn",
    )(lstart.reshape(-1), counts.reshape(-1), dest[0], dest[1], w[0], w[1], h2lin, wgu, wd)


def _l1_in_kernel(x_ref, mo_ref, mod0_ref, mod1_ref, g_ref, win_ref, vg_ref, ws_ref, bst_ref,
                  wg2_ref, bg_ref,
                  x2_ref, c_ref, q_ref, k_ref, v_ref, r_ref, la_ref):
    tm = x_ref.shape[1]
    m1 = mod1_ref[0]
    x2 = x_ref[0] + mod0_ref[0][5:6] * _from_lin(mo_ref, tm, x_ref.shape[2] // LANES)
    x2_ref[0] = x2
    h = _rms(x2, g_ref[...]) * (1.0 + m1[1:2]) + m1[0:1]
    proj = _dot(h.astype(BF16), win_ref[...])
    u = _gelu_tanh(proj[:, 0:512])
    vn = _rms(_gelu_tanh(proj[:, 512:1024]), vg_ref[...]).astype(BF16)
    q_ref[0] = proj[:, 1024:1280].astype(BF16)
    k_ref[0] = proj[:, 1280:1536].astype(BF16)
    v_ref[0] = proj[:, 1536:2048].astype(BF16)
    r_ref[0] = proj[:, 2048:2560].astype(BF16)
    z = _dot(proj[:, 2560:2688].astype(BF16), wg2_ref[...]) + bg_ref[...]
    la_ref[0] = (jnp.minimum(z, 0.0) - jnp.log(1.0 + jnp.exp(-jnp.abs(z)))) * (1.0 / GLA_GATE_TAU)

    rr = lax.broadcasted_iota(jnp.int32, (SG_CHUNK, SG_CHUNK), 0)
    cc = lax.broadcasted_iota(jnp.int32, (SG_CHUNK, SG_CHUNK), 1)
    gd = 512 // SG_GROUPS
    for g in range(SG_GROUPS):
        wt = jnp.where(rr >= cc, ws_ref[g], 0.0).astype(BF16)
        bcol = bst_ref[:, g:g + 1]
        for ch in range(tm // SG_CHUNK):
            rows = slice(ch * SG_CHUNK, (ch + 1) * SG_CHUNK)
            cols = slice(g * gd, (g + 1) * gd)
            mixed = _dot(wt, vn[rows, cols]) + bcol
            c_ref[0, rows, cols] = (u[rows, cols] * mixed).astype(BF16)


def _l1_in(x, mo, mod0, mod1, g, win, vg, ws, bst, wg2, bg):
    b, s, d = x.shape
    tm = ROW_TILE
    full = lambda a: pl.BlockSpec(a.shape, lambda i, j: (0,) * a.ndim)
    row = lambda w: pl.BlockSpec((1, tm, w), lambda i, j: (i, j, 0))
    modspec = pl.BlockSpec((1, 6, d), lambda i, j: (i, 0, 0))
    widths = [d, 512, GLA_HEADS * GLA_DK, GLA_HEADS * GLA_DK, GLA_HEADS * GLA_DV, GLA_HEADS * GLA_DV,
              GLA_HEADS * GLA_DK]
    dtypes = [F32, BF16, BF16, BF16, BF16, BF16, F32]
    return pl.pallas_call(
        _l1_in_kernel,
        grid=(b, s // tm),
        in_specs=[row(d), pl.BlockSpec((tm * (d // LANES), LANES), lambda i, j: (i * (s // tm) + j, 0)), modspec, modspec, full(g), full(win), full(vg), full(ws),
                  full(bst), full(wg2), full(bg)],
        out_specs=[row(w) for w in widths],
        out_shape=[jax.ShapeDtypeStruct((b, s, w), dt) for w, dt in zip(widths, dtypes)],
        compiler_params=_params(("arbitrary", "arbitrary")),
        name="l1_in",
    )(x, mo, mod0, mod1, g, win, vg, ws, bst, wg2, bg)


def _gla_kernel(q_ref, k_ref, v_ref, r_ref, la_ref, gn_ref, mask_ref, o_ref, state):
    j = pl.program_id(1)
    tm = q_ref.shape[1]
    ck = GLA_CHUNK
    n_ch = tm // ck
    nt = (((1,), (1,)), ((), ()))
    tn = (((0,), (0,)), ((), ()))

    @pl.when(j == 0)
    def _():
        state[...] = jnp.zeros_like(state)

    rr = lax.broadcasted_iota(jnp.int32, (ck, ck), 0)
    cc = lax.broadcasted_iota(jnp.int32, (ck, ck), 1)
    tril = jnp.where(rr >= cc, 1.0, 0.0).astype(BF16)
    q = q_ref[0].astype(F32) * (GLA_DK ** -0.5)
    k = k_ref[0].astype(F32)
    la = la_ref[0]
    la_parts = []
    for _ in range(3):
        part = la.astype(BF16)
        la_parts.append(part)
        la = la - part.astype(F32)
    q_dec, k_dec, k_end, b_last = [], [], [], []
    for ch in range(n_ch):
        rows = slice(ch * ck, (ch + 1) * ck)
        bc = _dot(tril, la_parts[0][rows]) + _dot(tril, la_parts[1][rows]) + _dot(tril, la_parts[2][rows])
        bl = bc[ck - 1:ck, :]
        b_last.append(bl)
        q_dec.append(q[rows] * jnp.exp(bc))
        k_dec.append(k[rows] * jnp.exp(-bc))
        k_end.append((k[rows] * jnp.exp(bl - bc)).astype(BF16))
    q_dec = jnp.concatenate(q_dec, axis=0)
    k_dec = jnp.concatenate(k_dec, axis=0).astype(BF16)
    decay_t = jnp.exp(jnp.concatenate(b_last, axis=0)).T
    lane = lax.broadcasted_iota(jnp.int32, (1, LANES), 1)
    mask = mask_ref[...] > 0.5
    srow = lax.broadcasted_iota(jnp.int32, (LANES, 2 * GLA_DV), 0)
    scol = lax.broadcasted_iota(jnp.int32, (LANES, 2 * GLA_DV), 1)
    diag = (srow < GLA_DK) == (scol < GLA_DV)
    for pair in range(GLA_HEADS // 2):
        ls = slice(pair * LANES, (pair + 1) * LANES)
        vs2 = slice(2 * pair * GLA_DV, (2 * pair + 2) * GLA_DV)
        qp = q_dec[:, ls].astype(BF16)
        st = state[pair]
        inter = []
        for ch in range(n_ch):
            rows = slice(ch * ck, (ch + 1) * ck)
            inter.append(_dot(qp[rows], st.astype(BF16)))
            kv = lax.dot_general(k_end[ch][:, ls], v_ref[0, rows, vs2], tn, preferred_element_type=F32)
            st = st * decay_t[ls, ch:ch + 1] + jnp.where(diag, kv, 0.0)
        state[pair] = st
        inter = jnp.concatenate(inter, axis=0)
        for sub in range(2):
            hd = 2 * pair + sub
            vs = slice(hd * GLA_DV, (hd + 1) * GLA_DV)
            mine = (lane >= sub * GLA_DK) & (lane < (sub + 1) * GLA_DK)
            qz = jnp.where(mine, q_dec[:, ls], 0.0).astype(BF16)
            intra = []
            for blk in range(tm // GLA_BLOCK):
                rows = slice(blk * GLA_BLOCK, (blk + 1) * GLA_BLOCK)
                attn = jnp.where(mask, lax.dot_general(qz[rows], k_dec[rows, ls], nt, preferred_element_type=F32), 0.0)
                intra.append(_dot(attn.astype(BF16), v_ref[0, rows, vs]))
            o = jnp.concatenate(intra, axis=0) + inter[:, sub * GLA_DV:(sub + 1) * GLA_DV]
            o = _rms(o, gn_ref[:, vs])
            o_ref[0, :, vs] = (_silu(r_ref[0, :, vs].astype(F32)) * o).astype(BF16)


def _gla(q, k, v, r, la, gn):
    b, s, _ = q.shape
    tm = ROW_TILE
    row = lambda w: pl.BlockSpec((1, tm, w), lambda i, j: (i, j, 0))
    pos = jnp.arange(GLA_BLOCK, dtype=jnp.int32)
    mask = ((pos[:, None] >= pos[None, :]) & (pos[:, None] // GLA_CHUNK == pos[None, :] // GLA_CHUNK)).astype(F32)
    return pl.pallas_call(
        _gla_kernel,
        grid=(b, s // tm),
        in_specs=[row(q.shape[2]), row(k.shape[2]), row(v.shape[2]), row(r.shape[2]), row(la.shape[2]),
                  pl.BlockSpec(gn.shape, lambda i, j: (0, 0)), pl.BlockSpec(mask.shape, lambda i, j: (0, 0))],
        out_specs=row(v.shape[2]),
        out_shape=jax.ShapeDtypeStruct(v.shape, BF16),
        scratch_shapes=[pltpu.VMEM((GLA_HEADS // 2, LANES, 2 * GLA_DV), F32)],
        compiler_params=_params(("arbitrary", "arbitrary")),
        name="gla",
    )(q, k, v, r, la, gn, mask)


def _final_kernel(x_ref, mo_ref, mod_ref, g_ref, o_ref):
    tm, d = x_ref.shape[1], x_ref.shape[2]
    x = x_ref[0] + mod_ref[0][5:6] * _from_lin(mo_ref, tm, d // LANES)
    o_ref[0] = _rms(x, g_ref[...])


def _final(x, mo, mod, g):
    b, s, d = x.shape
    tm = ROW_TILE
    row = lambda w: pl.BlockSpec((1, tm, w), lambda i, j: (i, j, 0))
    return pl.pallas_call(
        _final_kernel,
        grid=(b, s // tm),
        in_specs=[row(d), pl.BlockSpec((tm * (d // LANES), LANES), lambda i, j: (i * (s // tm) + j, 0)), pl.BlockSpec((1, 6, d), lambda i, j: (i, 0, 0)),
                  pl.BlockSpec(g.shape, lambda i, j: (0, 0))],
        out_specs=row(d),
        out_shape=jax.ShapeDtypeStruct((b, s, d), F32),
        compiler_params=_params(("arbitrary", "arbitrary")),
        name="final_norm",
    )(x, mo, mod, g)


def _rope_tables(positions):
    inv = 1.0 / (ROPE_BASE ** (jnp.arange(0, MLA_ROPE, 2, dtype=F32) / MLA_ROPE))
    ang = positions.astype(F32)[..., None] * inv
    z = lambda w: jnp.zeros(positions.shape + (w,), F32)
    cos, sin = lax.optimization_barrier((jnp.cos(ang), jnp.sin(ang)))
    return jnp.concatenate([z(MLA_NOPE), cos, sin, z(HEAD_PAD - MLA_NOPE - MLA_ROPE)], axis=-1)


def _pad_heads(w, per_head, keep):
    k = w.shape[0]
    w = w.reshape(k, MLA_HEADS, per_head)[:, :, :keep]
    w = jnp.pad(w, ((0, 0), (0, 0), (0, HEAD_PAD - keep)))
    return w.reshape(k, MLA_HEADS * HEAD_PAD)


def _router_weights(w_group, b_group, w_expert, b_expert):
    d = w_group.shape[0]
    pad = ROUTE_LANES - N_GROUPS - N_EXPERTS
    wr = jnp.concatenate([w_group, w_expert, jnp.zeros((d, pad), F32)], axis=1).astype(BF16)
    br = jnp.concatenate([b_group, b_expert, jnp.zeros((pad,), F32)])[None, :]
    return wr, br


def kernel(x, c, positions, adaln_w, adaln_b, norm_mix_g, norm_ffn_g, ab_w_in, mla_q_norm_g, mla_w_uq,
           mla_kv_norm_g, mla_w_ukv, conv_w, ab_w_out, cd_w_in, sg_v_norm_g, sg_w_s, sg_b_s, gla_w_g2, gla_b_g,
           gla_norm_g, cd_w_out, moe_w_group, moe_b_group, moe_w_expert, moe_b_expert, moe_w_gate_up, moe_w_down,
           final_norm_g):
    b, s, d = x.shape
    n = b * s
    tiles_per_seq = s // ROW_TILE
    mod = _adaln_mod(c, adaln_w, adaln_b)
    rt = _rope_tables(positions)

    w = ab_w_in[0]
    q_w, kv_w, kr_w, b_w, c_w, x_w = jnp.split(w, [256, 384, 416, 928, 1440], axis=1)
    kr_w = jnp.pad(kr_w, ((0, 0), (MLA_NOPE, HEAD_PAD - MLA_NOPE - MLA_ROPE)))
    win0 = jnp.concatenate([q_w, kv_w, kr_w, b_w, c_w, x_w], axis=1).astype(BF16)
    wuq = _pad_heads(mla_w_uq[0], MLA_NOPE + MLA_ROPE, MLA_NOPE + MLA_ROPE).astype(BF16)
    wukv = mla_w_ukv[0].reshape(MLA_KV_RANK, MLA_HEADS, MLA_NOPE + MLA_V)
    wuk = _pad_heads(wukv[:, :, :MLA_NOPE].reshape(MLA_KV_RANK, -1), MLA_NOPE, MLA_NOPE).astype(BF16)
    wuv = _pad_heads(wukv[:, :, MLA_NOPE:].reshape(MLA_KV_RANK, -1), MLA_V, MLA_V).T.astype(BF16)
    q, k, v, bconv = _l0_in(x, mod[0], norm_mix_g[0][None], win0, mla_q_norm_g[0][None], wuq,
                            mla_kv_norm_g[0][None], wuk, wuv, rt, conv_w[0])
    a = _attention(q, k, v)
    wout = ab_w_out[0].astype(BF16)
    wr, br = _router_weights(moe_w_group[0], moe_b_group[0], moe_w_expert[0], moe_b_expert[0])
    x1, h2, route0, cnt0 = _outproj_router(a.reshape(n, -1), bconv.reshape(n, -1), x.reshape(n, d), mod[0],
                                           wout[:512], wout[512:], norm_ffn_g[0][None], wr, br, tiles_per_seq)
    mo0 = _moe(h2, route0, cnt0, moe_w_gate_up[0].astype(BF16), moe_w_down[0].astype(BF16))

    w = cd_w_in[0]
    u_w, v_w, gq_w, gk_w, gv_w, gl_w, gr_w = jnp.split(w, [512, 1024, 1280, 1536, 2048, 2064], axis=1)
    gl_w = jnp.pad(gl_w, ((0, 0), (0, 128 - GLA_GATE_RANK)))
    win1 = jnp.concatenate([u_w, v_w, gq_w, gk_w, gv_w, gr_w, gl_w], axis=1).astype(BF16)
    wg2 = jnp.pad(gla_w_g2[0], ((0, 128 - GLA_GATE_RANK), (0, 0))).astype(BF16)
    x2, cout, gq, gk, gv, gr, la = _l1_in(
        x1.reshape(b, s, d), mo0, mod[0], mod[1], norm_mix_g[1][None], win1, sg_v_norm_g[0][None], sg_w_s[0], sg_b_s[0].T, wg2, gla_b_g[0][None])
    dout = _gla(gq, gk, gv, gr, la, gla_norm_g[0][None])
    wout = cd_w_out[0].astype(BF16)
    wr, br = _router_weights(moe_w_group[1], moe_b_group[1], moe_w_expert[1], moe_b_expert[1])
    x3, h2, route1, cnt1 = _outproj_router(cout.reshape(n, -1), dout.reshape(n, -1), x2.reshape(n, d), mod[1],
                                           wout[:512], wout[512:], norm_ffn_g[1][None], wr, br, tiles_per_seq)
    mo1 = _moe(h2, route1, cnt1, moe_w_gate_up[1].astype(BF16), moe_w_down[1].astype(BF16))
    return _final(x3.reshape(b, s, d), mo1, mod[1], final_norm_g[None])
```

```python
import functools

import jax
import jax.numpy as jnp
from jax import lax
from jax.experimental import pallas as pl
from jax.experimental.pallas import tpu as pltpu

F32 = jnp.float32
BF16 = jnp.bfloat16
HIGHEST = lax.Precision.HIGHEST

EPS = 1e-6
MLA_HEADS = 8
MLA_NOPE = 64
MLA_ROPE = 32
MLA_V = 64
MLA_Q_RANK = 256
MLA_KV_RANK = 128
ROPE_BASE = 10000.0
HEAD_PAD = 128
CONV_K = 3
SG_GROUPS = 4
SG_CHUNK = 128
GLA_HEADS = 4
GLA_DK = 64
GLA_DV = 128
GLA_GATE_RANK = 16
GLA_GATE_TAU = 16.0
GLA_CHUNK = 64
GLA_BLOCK = 256
N_GROUPS = 4
EXPERTS_PER_GROUP = 8
N_EXPERTS = N_GROUPS * EXPERTS_PER_GROUP
D_EXPERT = 256
MOE_TILE = 2048
MOE_CHUNK = 160
ROUTE_REC = 8
MOE_EXPERTS_PER_STEP = 2
MOE_ROW_UNROLL = 32
LANES = 128
ROUTE_LANES = 128
EXPERT_LANE0 = N_GROUPS

ROW_TILE = 512
ATTN_TK = 512
ATTN_HEADS = 4
VMEM_LIMIT = 56 * 1024 * 1024
MOE_VMEM_LIMIT = 60 * 1024 * 1024


def _rms(x, g):
    return x * lax.rsqrt(jnp.mean(x * x, axis=-1, keepdims=True) + EPS) * g


def _silu(x):
    return x * jax.nn.sigmoid(x)


def _gelu_tanh(x):
    return 0.5 * x * (1.0 + jnp.tanh(0.7978845608028654 * (x + 0.044715 * (x * x * x))))


def _dot(a, b):
    return jnp.dot(a, b, preferred_element_type=F32)


def _params(sem):
    return pltpu.CompilerParams(dimension_semantics=sem, vmem_limit_bytes=VMEM_LIMIT)


def _to_lin(ref, val, lead=()):
    r, d = val.shape
    nb = d // LANES
    for jb in range(nb):
        ref[lead + (pl.ds(jb, r, stride=nb), slice(None))] = val[:, jb * LANES:(jb + 1) * LANES]


def _from_lin(ref, r, nb, lead=()):
    return jnp.concatenate([ref[lead + (pl.ds(jb, r, stride=nb), slice(None))] for jb in range(nb)], axis=1)


def _adaln_kernel(c_ref, w_ref, b_ref, o_ref):
    c = c_ref[...]
    o_ref[0] = jnp.dot(_silu(c), w_ref[0], preferred_element_type=F32, precision=HIGHEST) + b_ref[0]


def _adaln_mod(c, adaln_w, adaln_b):
    depth, d, d6 = adaln_w.shape
    b = c.shape[0]
    n_chunks = d6 // d
    out = pl.pallas_call(
        _adaln_kernel,
        grid=(depth, n_chunks),
        in_specs=[
            pl.BlockSpec((b, d), lambda l, j: (0, 0)),
            pl.BlockSpec((1, d, d), lambda l, j: (l, 0, j)),
            pl.BlockSpec((1, 1, d), lambda l, j: (l, 0, j)),
        ],
        out_specs=pl.BlockSpec((1, b, d), lambda l, j: (l, 0, j)),
        out_shape=jax.ShapeDtypeStruct((depth, b, d6), F32),
        compiler_params=_params(("arbitrary", "arbitrary")),
        name="adaln_mod",
    )(c, adaln_w, adaln_b.reshape(depth, 1, d6))
    return out.reshape(depth, b, n_chunks, d)


def _rope(blk, c, s1, s2):
    return blk * c + pltpu.roll(blk, HEAD_PAD - MLA_ROPE // 2, 1) * s1 + pltpu.roll(blk, MLA_ROPE // 2, 1) * s2


def _l0_in_kernel(x_ref, mod_ref, g_ref, win_ref, qg_ref, wuq_ref, kvg_ref, wuk_ref, wuv_ref,
                  rt_ref, cw_ref,
                  q_ref, k_ref, v_ref, bc_ref, ubuf):
    j = pl.program_id(1)
    tm = x_ref.shape[1]
    x = x_ref[0]
    m = mod_ref[0]
    h = _rms(x, g_ref[...]) * (1.0 + m[1:2]) + m[0:1]
    proj = _dot(h.astype(BF16), win_ref[...])
    q_lat = proj[:, 0:256]
    kv_lat = proj[:, 256:384]
    kr = proj[:, 384:512]
    bg = proj[:, 512:1024]
    cg = proj[:, 1024:1536]
    xi = proj[:, 1536:2048]

    half = MLA_ROPE // 2
    tab = rt_ref[0]
    lane = lax.broadcasted_iota(jnp.int32, tab.shape, 1)
    in_x1 = (lane >= MLA_NOPE) & (lane < MLA_NOPE + half)
    in_x2 = (lane >= MLA_NOPE + half) & (lane < MLA_NOPE + MLA_ROPE)
    c = jnp.where(lane < MLA_NOPE, 1.0, jnp.where(in_x1, tab, jnp.where(in_x2, pltpu.roll(tab, half, 1), 0.0)))
    s1 = jnp.where(in_x1, -pltpu.roll(tab, HEAD_PAD - half, 1), 0.0)
    s2 = jnp.where(in_x2, tab, 0.0)
    scale = (MLA_NOPE + MLA_ROPE) ** -0.5 * 1.4426950408889634
    q = _dot(_rms(q_lat, qg_ref[...]).astype(BF16), wuq_ref[...])
    kvn = _rms(kv_lat, kvg_ref[...]).astype(BF16)
    kn = _dot(kvn, wuk_ref[...])
    vt = lax.dot_general(wuv_ref[...], kvn, (((1,), (1,)), ((), ())), preferred_element_type=F32)
    vrow = lax.broadcasted_iota(jnp.int32, vt.shape, 0)
    vt = jnp.where((vrow & (HEAD_PAD - 1)) == MLA_V, 1.0, vt).astype(BF16)
    for cb in range(tm // ATTN_TK):
        v_ref[0, cb] = vt[:, cb * ATTN_TK:(cb + 1) * ATTN_TK]
    krr = _rope(kr, c, s1, s2)
    for hd in range(MLA_HEADS):
        sl = slice(hd * HEAD_PAD, (hd + 1) * HEAD_PAD)
        q_ref[0, :, sl] = (_rope(q[:, sl], c, s1, s2) * scale).astype(BF16)
        k_ref[0, :, sl] = (kn[:, sl] + krr).astype(BF16)

    u = cg * xi

    @pl.when(j == 0)
    def _():
        ubuf[0:8, :] = jnp.zeros((8, ubuf.shape[1]), F32)

    ubuf[8:tm + 8, :] = u
    r1 = ubuf[7:tm + 7, :]
    r2 = ubuf[6:tm + 6, :]
    cw = cw_ref[...]
    y = cw[2:3] * u + cw[1:2] * r1 + cw[0:1] * r2
    bc_ref[0] = (bg * y).astype(BF16)
    ubuf[0:8, :] = u[tm - 8:tm, :]


def _l0_in(x, mod, g, win, qg, wuq, kvg, wuk, wuv, rt, cw):
    b, s, d = x.shape
    tm = ROW_TILE
    full = lambda a: pl.BlockSpec(a.shape, lambda i, j: (0,) * a.ndim)
    row = lambda w: pl.BlockSpec((1, tm, w), lambda i, j: (i, j, 0))
    return pl.pallas_call(
        _l0_in_kernel,
        grid=(b, s // tm),
        in_specs=[row(d), pl.BlockSpec((1, 6, d), lambda i, j: (i, 0, 0)), full(g), full(win), full(qg), full(wuq),
                  full(kvg), full(wuk), full(wuv), row(HEAD_PAD), full(cw)],
        out_specs=[row(MLA_HEADS * HEAD_PAD), row(MLA_HEADS * HEAD_PAD),
                   pl.BlockSpec((1, tm // ATTN_TK, MLA_HEADS * HEAD_PAD, ATTN_TK), lambda i, j: (i, j, 0, 0)),
                   row(d // 2)],
        out_shape=[jax.ShapeDtypeStruct((b, s, MLA_HEADS * HEAD_PAD), BF16),
                   jax.ShapeDtypeStruct((b, s, MLA_HEADS * HEAD_PAD), BF16),
                   jax.ShapeDtypeStruct((b, s // ATTN_TK, MLA_HEADS * HEAD_PAD, ATTN_TK), BF16),
                   jax.ShapeDtypeStruct((b, s, d // 2), BF16)],
        scratch_shapes=[pltpu.VMEM((tm + 8, d // 2), F32)],
        compiler_params=_params(("arbitrary", "arbitrary")),
        name="l0_in",
    )(x, mod, g, win, qg, wuq, kvg, wuk, wuv, rt, cw)


def _attn_kernel(q_ref, k_ref, v_ref, o_ref, s_buf, m_buf, acc_buf):
    qi = pl.program_id(2)
    tq = q_ref.shape[1]
    tk = ATTN_TK
    neg = jnp.finfo(F32).min
    heads = range(ATTN_HEADS)
    hsl = [slice(hd * HEAD_PAD, (hd + 1) * HEAD_PAD) for hd in heads]

    def scores(jj, hd):
        off = pl.multiple_of(jj * tk, tk)
        return lax.dot_general(k_ref[0, pl.ds(off, tk), hsl[hd]], q_ref[0, :, hsl[hd]], (((1,), (1,)), ((), ())),
                               preferred_element_type=F32)

    def consume(jj, s, hd, mask):
        if mask is not None:
            s = jnp.where(mask, s, neg)
        m = m_buf[hd]
        m_new = jnp.maximum(m, jnp.max(s, axis=0, keepdims=True))
        p = jnp.exp2(s - m_new).astype(BF16)
        pv = _dot(v_ref[0, jj, hsl[hd], :], p)
        acc_buf[hd] = jnp.exp2(m - m_new) * acc_buf[hd] + pv
        m_buf[hd] = m_new

    for hd in heads:
        m_buf[hd] = jnp.full((1, tq), neg, F32)
        acc_buf[hd] = jnp.zeros((HEAD_PAD, tq), F32)
        s_buf[0, hd] = scores(0, hd)

    def advance(jj, cur):
        for hd in heads:
            s_buf[1 - cur, hd] = scores(jj + 1, hd)
        for hd in heads:
            consume(jj, s_buf[cur, hd], hd, None)

    def body(pp, c):
        advance(2 * pp, 0)
        advance(2 * pp + 1, 1)
        return c

    lax.fori_loop(0, qi // 2, body, 0)
    krow = lax.broadcasted_iota(jnp.int32, (tk, tq), 0)
    qcol = lax.broadcasted_iota(jnp.int32, (tk, tq), 1)
    odd = (qi & 1) == 1

    @pl.when(odd)
    def _():
        advance(qi - 1, 0)
        for hd in heads:
            consume(qi, s_buf[1, hd], hd, qcol >= krow)

    @pl.when(jnp.logical_not(odd))
    def _():
        for hd in heads:
            consume(qi, s_buf[0, hd], hd, qcol >= krow)
    o = jnp.concatenate([acc_buf[hd][0:MLA_V] / acc_buf[hd][MLA_V:MLA_V + 1] for hd in heads], axis=0)
    o_ref[0] = o.T.astype(BF16)


def _attention(q, k, v):
    b, s, _ = q.shape
    tq = tk = ATTN_TK
    nh = ATTN_HEADS
    return pl.pallas_call(
        _attn_kernel,
        grid=(b, MLA_HEADS // nh, s // tq),
        in_specs=[pl.BlockSpec((1, tq, nh * HEAD_PAD), lambda i, h, j: (i, j, h)),
                  pl.BlockSpec((1, s, nh * HEAD_PAD), lambda i, h, j: (i, 0, h)),
                  pl.BlockSpec((1, s // tk, nh * HEAD_PAD, tk), lambda i, h, j: (i, 0, h, 0))],
        out_specs=pl.BlockSpec((1, tq, nh * MLA_V), lambda i, h, j: (i, j, h)),
        out_shape=jax.ShapeDtypeStruct((b, s, MLA_HEADS * MLA_V), BF16),
        scratch_shapes=[pltpu.VMEM((2, nh, tk, tq), F32), pltpu.VMEM((nh, 1, tq), F32),
                        pltpu.VMEM((nh, HEAD_PAD, tq), F32)],
        compiler_params=_params(("arbitrary", "arbitrary", "arbitrary")),
        name="mla_attention",
    )(q, k, v)


def _outproj_router_kernel(a_ref, b_ref, x_ref, mod_ref, wa_ref, wb_ref, g_ref, wr_ref, br_ref, stril_ref,
                           x1_ref, h2_ref, route_ref, cnt_ref, carry):
    i = pl.program_id(0)
    tm = x_ref.shape[0]
    m = mod_ref[0]
    mix = _dot(a_ref[...], wa_ref[...]) + _dot(b_ref[...], wb_ref[...])
    x1 = x_ref[...] + m[2:3] * mix
    x1_ref[...] = x1
    h2 = _rms(x1, g_ref[...]) * (1.0 + m[4:5]) + m[3:4]
    _to_lin(h2_ref, h2)
    logits = _dot(h2.astype(BF16), wr_ref[...]) + br_ref[...]

    neg = jnp.finfo(F32).min
    lane = lax.broadcasted_iota(jnp.int32, (tm, ROUTE_LANES), 1).astype(F32)
    big = float(ROUTE_LANES)
    is_grp = lane < N_GROUPS
    lg = jnp.where(is_grp, logits, neg)
    gmax = jnp.max(lg, axis=-1, keepdims=True)
    gsel = jnp.min(jnp.where(lg == gmax, lane, big), axis=-1, keepdims=True)
    pg = 1.0 / jnp.sum(jnp.where(is_grp, jnp.exp(logits - gmax), 0.0), axis=-1, keepdims=True)
    lo = EXPERT_LANE0 + EXPERTS_PER_GROUP * gsel
    le = jnp.where((lane >= lo) & (lane < lo + EXPERTS_PER_GROUP), logits, neg)
    v0 = jnp.max(le, axis=-1, keepdims=True)
    l0 = jnp.min(jnp.where(le == v0, lane, big), axis=-1, keepdims=True)
    le2 = jnp.where(lane == l0, neg, le)
    v1 = jnp.max(le2, axis=-1, keepdims=True)
    l1 = jnp.min(jnp.where(le2 == v1, lane, big), axis=-1, keepdims=True)
    t = jnp.exp(v1 - v0)
    w0 = pg / (1.0 + t)
    w1 = pg * t / (1.0 + t)

    @pl.when(i % (MOE_TILE // tm) == 0)
    def _():
        carry[...] = jnp.zeros_like(carry)

    sel0 = lane == l0
    sel1 = lane == l1
    oh = jnp.where(sel0 | sel1, 1.0, 0.0)
    cum = _dot(stril_ref[...], oh.astype(BF16)) + carry[...]
    rank0 = jnp.sum(jnp.where(sel0, cum, 0.0), axis=-1, keepdims=True)
    rank1 = jnp.sum(jnp.where(sel1, cum, 0.0), axis=-1, keepdims=True)
    carry[...] = carry[...] + jnp.sum(oh, axis=0, keepdims=True)
    cnt_ref[0] = carry[...]

    rec = jnp.where(lane == 0, l0 - EXPERT_LANE0, 0.0)
    rec = jnp.where(lane == 1, l1 - EXPERT_LANE0, rec)
    rec = jnp.where(lane == 2, rank0, rec)
    rec = jnp.where(lane == 3, rank1, rec)
    rec = jnp.where(lane == 4, w0, rec)
    rec = jnp.where(lane == 5, w1, rec)
    route_ref[...] = rec.T[:ROUTE_REC, :]


def _outproj_router(a, bb, x, mod, wa, wb, g, wr, br, tiles_per_seq):
    n, d = x.shape
    tm = ROW_TILE
    nb = d // LANES
    full = lambda arr: pl.BlockSpec(arr.shape, lambda i: (0,) * arr.ndim)
    row = lambda w: pl.BlockSpec((tm, w), lambda i: (i, 0))
    pos = jnp.arange(tm, dtype=jnp.int32)
    stril = (pos[:, None] > pos[None, :]).astype(BF16)
    return pl.pallas_call(
        _outproj_router_kernel,
        grid=(n // tm,),
        in_specs=[row(a.shape[1]), row(bb.shape[1]), row(d),
                  pl.BlockSpec((1, 6, d), lambda i: (i // tiles_per_seq, 0, 0)),
                  full(wa), full(wb), full(g), full(wr), full(br), full(stril)],
        out_specs=[row(d), pl.BlockSpec((tm * nb, LANES), lambda i: (i, 0)), pl.BlockSpec((ROUTE_REC, tm), lambda i: (0, i)),
                   pl.BlockSpec((1, 1, ROUTE_LANES), lambda i: (i // (MOE_TILE // tm), 0, 0))],
        out_shape=[jax.ShapeDtypeStruct((n, d), F32), jax.ShapeDtypeStruct((n * nb, LANES), F32),
                   jax.ShapeDtypeStruct((ROUTE_REC, n), F32),
                   jax.ShapeDtypeStruct((n // MOE_TILE, 1, ROUTE_LANES), F32)],
        scratch_shapes=[pltpu.VMEM((1, ROUTE_LANES), F32)],
        compiler_params=_params(("arbitrary",)),
        name="outproj_router",
    )(a, bb, x, mod, wa, wb, g, wr, br, stril)


def _moe_kernel(ls_ref, cn_ref, d0_ref, d1_ref, w0_ref, w1_ref, h_ref, wgu_ref, wd_ref, o_ref, sbuf, ybuf):
    i = pl.program_id(0)
    ep = pl.program_id(1)
    nb = h_ref.shape[0] // MOE_TILE
    row_at = lambda r: pl.ds(pl.multiple_of(r * nb, nb), nb)
    lin_at = lambda a: pl.ds(pl.multiple_of(a, nb), nb)

    @pl.when((i == 0) & (ep == 0))
    def _():
        sbuf[...] = jnp.zeros(sbuf.shape, F32)

    @pl.when(ep == 0)
    def _():
        def dispatch(g, c):
            for k in range(MOE_ROW_UNROLL):
                t = g * MOE_ROW_UNROLL + k
                row = h_ref[row_at(t), :]
                sbuf[lin_at(d0_ref[0, 0, t]), :] = row
                sbuf[lin_at(d1_ref[0, 0, t]), :] = row
            return c

        lax.fori_loop(0, MOE_TILE // MOE_ROW_UNROLL, dispatch, 0)

    subs = range(MOE_EXPERTS_PER_STEP)
    starts = [ls_ref[i * N_EXPERTS + ep * MOE_EXPERTS_PER_STEP + sub] for sub in subs]
    cnts = [cn_ref[i * N_EXPERTS + ep * MOE_EXPERTS_PER_STEP + sub] for sub in subs]

    def ffn(base, sub):
        x = jnp.concatenate(
            [sbuf[pl.ds(pl.multiple_of(base * nb, nb) + jb, MOE_CHUNK, stride=nb), :] for jb in range(nb)], axis=1)
        gu = _dot(x.astype(BF16), wgu_ref[sub])
        hmid = _silu(gu[:, :D_EXPERT]) * gu[:, D_EXPERT:]
        return _dot(hmid.astype(BF16), wd_ref[sub])

    def write_back(base, rows, sub):
        def group(g, cc):
            yg = ybuf[sub, pl.ds(pl.multiple_of(g * 8, 8), 8), :]
            row0 = pl.multiple_of((base + g * 8) * nb, nb)
            for jb in range(nb):
                sbuf[pl.ds(row0 + jb, 8, stride=nb), :] = yg[:, jb * LANES:(jb + 1) * LANES]
            return cc

        def group4(g4, cc):
            for u in range(4):
                group(g4 * 4 + u, cc)
            return cc

        n8 = lax.shift_right_logical(jnp.clip(rows, 0, MOE_CHUNK) + 7, 3)
        n4 = lax.shift_right_logical(n8, 2)
        lax.fori_loop(0, n4, group4, 0)
        lax.fori_loop(n4 * 4, n8, group, 0)

    ys = [ffn(starts[sub], sub) for sub in subs]
    for sub in subs:
        ybuf[sub] = ys[sub]
    for sub in subs:
        write_back(starts[sub], cnts[sub], sub)

    for sub in subs:
        def chunk(c, carry, sub=sub):
            base = starts[sub] + c * MOE_CHUNK
            ybuf[sub] = ffn(base, sub)
            write_back(base, cnts[sub] - c * MOE_CHUNK, sub)
            return carry

        lax.fori_loop(1, (cnts[sub] + (MOE_CHUNK - 1)) // MOE_CHUNK, chunk, 0)

    @pl.when(ep == N_EXPERTS // MOE_EXPERTS_PER_STEP - 1)
    def _():
        def combine(g, c):
            for k in range(MOE_ROW_UNROLL):
                t = g * MOE_ROW_UNROLL + k
                y0 = sbuf[lin_at(d0_ref[0, 0, t]), :]
                y1 = sbuf[lin_at(d1_ref[0, 0, t]), :]
                o_ref[row_at(t), :] = w0_ref[0, 0, t] * y0 + w1_ref[0, 0, t] * y1
            return c

        lax.fori_loop(0, MOE_TILE // MOE_ROW_UNROLL, combine, 0)


def _moe(h2lin, route, cnt, wgu, wd):
    n_tiles = cnt.shape[0]
    d = wgu.shape[1]
    nb = d // LANES
    pairs = 2 * MOE_TILE
    e = route[0:2].astype(jnp.int32).reshape(2, n_tiles, MOE_TILE)
    rank = route[2:4].astype(jnp.int32).reshape(2, n_tiles, MOE_TILE)
    w = route[4:6].reshape(2, n_tiles, 1, MOE_TILE)
    counts = cnt[:, 0, EXPERT_LANE0:EXPERT_LANE0 + N_EXPERTS].astype(jnp.int32)
    padded = (counts + 7) // 8 * 8
    lstart = jnp.cumsum(padded, axis=1) - padded
    experts = jnp.arange(N_EXPERTS, dtype=jnp.int32)[:, None, None, None]
    seg = jnp.sum(jnp.where(e[None] == experts, lstart.T[:, None, :, None], 0), axis=0)
    dest = ((seg + rank) * nb)[:, :, None, :]
    sorted_rows = pairs + 8 * N_EXPERTS + MOE_CHUNK

    eps = MOE_EXPERTS_PER_STEP
    smem = lambda: pl.BlockSpec((1, 1, MOE_TILE), lambda i, e, ls, cn: (i, 0, 0), memory_space=pltpu.SMEM)
    tile = lambda: pl.BlockSpec((MOE_TILE * nb, LANES), lambda i, e, ls, cn: (i, 0))
    grid_spec = pltpu.PrefetchScalarGridSpec(
        num_scalar_prefetch=2,
        grid=(n_tiles, N_EXPERTS // eps),
        in_specs=[smem(), smem(), smem(), smem(), tile(),
                  pl.BlockSpec((eps, d, 2 * D_EXPERT), lambda i, e, ls, cn: (e, 0, 0)),
                  pl.BlockSpec((eps, D_EXPERT, d), lambda i, e, ls, cn: (e, 0, 0))],
        out_specs=tile(),
        scratch_shapes=[pltpu.VMEM((sorted_rows * nb, LANES), F32), pltpu.VMEM((eps, MOE_CHUNK, d), F32)],
    )
    return pl.pallas_call(
        _moe_kernel,
        grid_spec=grid_spec,
        out_shape=jax.ShapeDtypeStruct((n_tiles * MOE_TILE * nb, LANES), F32),
        compiler_params=pltpu.CompilerParams(dimension_semantics=("arbitrary", "arbitrary"),
                                             vmem_limit_bytes=MOE_VMEM_LIMIT),
        name="moe_ffn",
    )(lstart.reshape(-1), counts.reshape(-1), dest[0], dest[1], w[0], w[1], h2lin, wgu, wd)


def _l1_in_kernel(x_ref, mo_ref, mod0_ref, mod1_ref, g_ref, win_ref, vg_ref, ws_ref, bst_ref,
                  wg2_ref, bg_ref,
                  x2_ref, c_ref, q_ref, k_ref, v_ref, r_ref, la_ref):
    tm = x_ref.shape[1]
    m1 = mod1_ref[0]
    x2 = x_ref[0] + mod0_ref[0][5:6] * _from_lin(mo_ref, tm, x_ref.shape[2] // LANES)
    x2_ref[0] = x2
    h = _rms(x2, g_ref[...]) * (1.0 + m1[1:2]) + m1[0:1]
    proj = _dot(h.astype(BF16), win_ref[...])
    u = _gelu_tanh(proj[:, 0:512])
    vn = _rms(_gelu_tanh(proj[:, 512:1024]), vg_ref[...]).astype(BF16)
    q_ref[0] = proj[:, 1024:1280].astype(BF16)
    k_ref[0] = proj[:, 1280:1536].astype(BF16)
    v_ref[0] = proj[:, 1536:2048].astype(BF16)
    r_ref[0] = proj[:, 2048:2560].astype(BF16)
    z = _dot(proj[:, 2560:2688].astype(BF16), wg2_ref[...]) + bg_ref[...]
    la_ref[0] = (jnp.minimum(z, 0.0) - jnp.log(1.0 + jnp.exp(-jnp.abs(z)))) * (1.0 / GLA_GATE_TAU)

    rr = lax.broadcasted_iota(jnp.int32, (SG_CHUNK, SG_CHUNK), 0)
    cc = lax.broadcasted_iota(jnp.int32, (SG_CHUNK, SG_CHUNK), 1)
    gd = 512 // SG_GROUPS
    for g in range(SG_GROUPS):
        wt = jnp.where(rr >= cc, ws_ref[g], 0.0).astype(BF16)
        bcol = bst_ref[:, g:g + 1]
        for ch in range(tm // SG_CHUNK):
            rows = slice(ch * SG_CHUNK, (ch + 1) * SG_CHUNK)
            cols = slice(g * gd, (g + 1) * gd)
            mixed = _dot(wt, vn[rows, cols]) + bcol
            c_ref[0, rows, cols] = (u[rows, cols] * mixed).astype(BF16)


def _l1_in(x, mo, mod0, mod1, g, win, vg, ws, bst, wg2, bg):
    b, s, d = x.shape
    tm = ROW_TILE
    full = lambda a: pl.BlockSpec(a.shape, lambda i, j: (0,) * a.ndim)
    row = lambda w: pl.BlockSpec((1, tm, w), lambda i, j: (i, j, 0))
    modspec = pl.BlockSpec((1, 6, d), lambda i, j: (i, 0, 0))
    widths = [d, 512, GLA_HEADS * GLA_DK, GLA_HEADS * GLA_DK, GLA_HEADS * GLA_DV, GLA_HEADS * GLA_DV,
              GLA_HEADS * GLA_DK]
    dtypes = [F32, BF16, BF16, BF16, BF16, BF16, F32]
    return pl.pallas_call(
        _l1_in_kernel,
        grid=(b, s // tm),
        in_specs=[row(d), pl.BlockSpec((tm * (d // LANES), LANES), lambda i, j: (i * (s // tm) + j, 0)), modspec, modspec, full(g), full(win), full(vg), full(ws),
                  full(bst), full(wg2), full(bg)],
        out_specs=[row(w) for w in widths],
        out_shape=[jax.ShapeDtypeStruct((b, s, w), dt) for w, dt in zip(widths, dtypes)],
        compiler_params=_params(("arbitrary", "arbitrary")),
        name="l1_in",
    )(x, mo, mod0, mod1, g, win, vg, ws, bst, wg2, bg)


def _gla_kernel(q_ref, k_ref, v_ref, r_ref, la_ref, gn_ref, mask_ref, o_ref, state):
    j = pl.program_id(1)
    tm = q_ref.shape[1]
    ck = GLA_CHUNK
    n_ch = tm // ck
    nt = (((1,), (1,)), ((), ()))
    tn = (((0,), (0,)), ((), ()))

    @pl.when(j == 0)
    def _():
        state[...] = jnp.zeros_like(state)

    rr = lax.broadcasted_iota(jnp.int32, (ck, ck), 0)
    cc = lax.broadcasted_iota(jnp.int32, (ck, ck), 1)
    tril = jnp.where(rr >= cc, 1.0, 0.0).astype(BF16)
    q = q_ref[0].astype(F32) * (GLA_DK ** -0.5)
    k = k_ref[0].astype(F32)
    la = la_ref[0]
    la_parts = []
    for _ in range(3):
        part = la.astype(BF16)
        la_parts.append(part)
        la = la - part.astype(F32)
    q_dec, k_dec, k_end, b_last = [], [], [], []
    for ch in range(n_ch):
        rows = slice(ch * ck, (ch + 1) * ck)
        bc = _dot(tril, la_parts[0][rows]) + _dot(tril, la_parts[1][rows]) + _dot(tril, la_parts[2][rows])
        bl = bc[ck - 1:ck, :]
        b_last.append(bl)
        q_dec.append(q[rows] * jnp.exp(bc))
        k_dec.append(k[rows] * jnp.exp(-bc))
        k_end.append((k[rows] * jnp.exp(bl - bc)).astype(BF16))
    q_dec = jnp.concatenate(q_dec, axis=0)
    k_dec = jnp.concatenate(k_dec, axis=0).astype(BF16)
    decay_t = jnp.exp(jnp.concatenate(b_last, axis=0)).T
    lane = lax.broadcasted_iota(jnp.int32, (1, LANES), 1)
    mask = mask_ref[...] > 0.5
    srow = lax.broadcasted_iota(jnp.int32, (LANES, 2 * GLA_DV), 0)
    scol = lax.broadcasted_iota(jnp.int32, (LANES, 2 * GLA_DV), 1)
    diag = (srow < GLA_DK) == (scol < GLA_DV)
    for pair in range(GLA_HEADS // 2):
        ls = slice(pair * LANES, (pair + 1) * LANES)
        vs2 = slice(2 * pair * GLA_DV, (2 * pair + 2) * GLA_DV)
        qp = q_dec[:, ls].astype(BF16)
        st = state[pair]
        inter = []
        for ch in range(n_ch):
            rows = slice(ch * ck, (ch + 1) * ck)
            inter.append(_dot(qp[rows], st.astype(BF16)))
            kv = lax.dot_general(k_end[ch][:, ls], v_ref[0, rows, vs2], tn, preferred_element_type=F32)
            st = st * decay_t[ls, ch:ch + 1] + jnp.where(diag, kv, 0.0)
        state[pair] = st
        inter = jnp.concatenate(inter, axis=0)
        for sub in range(2):
            hd = 2 * pair + sub
            vs = slice(hd * GLA_DV, (hd + 1) * GLA_DV)
            mine = (lane >= sub * GLA_DK) & (lane < (sub + 1) * GLA_DK)
            qz = jnp.where(mine, q_dec[:, ls], 0.0).astype(BF16)
            intra = []
            for blk in range(tm // GLA_BLOCK):
                rows = slice(blk * GLA_BLOCK, (blk + 1) * GLA_BLOCK)
                attn = jnp.where(mask, lax.dot_general(qz[rows], k_dec[rows, ls], nt, preferred_element_type=F32), 0.0)
                intra.append(_dot(attn.astype(BF16), v_ref[0, rows, vs]))
            o = jnp.concatenate(intra, axis=0) + inter[:, sub * GLA_DV:(sub + 1) * GLA_DV]
            o = _rms(o, gn_ref[:, vs])
            o_ref[0, :, vs] = (_silu(r_ref[0, :, vs].astype(F32)) * o).astype(BF16)


def _gla(q, k, v, r, la, gn):
    b, s, _ = q.shape
    tm = ROW_TILE
    row = lambda w: pl.BlockSpec((1, tm, w), lambda i, j: (i, j, 0))
    pos = jnp.arange(GLA_BLOCK, dtype=jnp.int32)
    mask = ((pos[:, None] >= pos[None, :]) & (pos[:, None] // GLA_CHUNK == pos[None, :] // GLA_CHUNK)).astype(F32)
    return pl.pallas_call(
        _gla_kernel,
        grid=(b, s // tm),
        in_specs=[row(q.shape[2]), row(k.shape[2]), row(v.shape[2]), row(r.shape[2]), row(la.shape[2]),
                  pl.BlockSpec(gn.shape, lambda i, j: (0, 0)), pl.BlockSpec(mask.shape, lambda i, j: (0, 0))],
        out_specs=row(v.shape[2]),
        out_shape=jax.ShapeDtypeStruct(v.shape, BF16),
        scratch_shapes=[pltpu.VMEM((GLA_HEADS // 2, LANES, 2 * GLA_DV), F32)],
        compiler_params=_params(("arbitrary", "arbitrary")),
        name="gla",
    )(q, k, v, r, la, gn, mask)


def _final_kernel(x_ref, mo_ref, mod_ref, g_ref, o_ref):
    tm, d = x_ref.shape[1], x_ref.shape[2]
    x = x_ref[0] + mod_ref[0][5:6] * _from_lin(mo_ref, tm, d // LANES)
    o_ref[0] = _rms(x, g_ref[...])


def _final(x, mo, mod, g):
    b, s, d = x.shape
    tm = ROW_TILE
    row = lambda w: pl.BlockSpec((1, tm, w), lambda i, j: (i, j, 0))
    return pl.pallas_call(
        _final_kernel,
        grid=(b, s // tm),
        in_specs=[row(d), pl.BlockSpec((tm * (d // LANES), LANES), lambda i, j: (i * (s // tm) + j, 0)), pl.BlockSpec((1, 6, d), lambda i, j: (i, 0, 0)),
                  pl.BlockSpec(g.shape, lambda i, j: (0, 0))],
        out_specs=row(d),
        out_shape=jax.ShapeDtypeStruct((b, s, d), F32),
        compiler_params=_params(("arbitrary", "arbitrary")),
        name="final_norm",
    )(x, mo, mod, g)


def _rope_tables(positions):
    inv = 1.0 / (ROPE_BASE ** (jnp.arange(0, MLA_ROPE, 2, dtype=F32) / MLA_ROPE))
    ang = positions.astype(F32)[..., None] * inv
    z = lambda w: jnp.zeros(positions.shape + (w,), F32)
    cos, sin = lax.optimization_barrier((jnp.cos(ang), jnp.sin(ang)))
    return jnp.concatenate([z(MLA_NOPE), cos, sin, z(HEAD_PAD - MLA_NOPE - MLA_ROPE)], axis=-1)


def _pad_heads(w, per_head, keep):
    k = w.shape[0]
    w = w.reshape(k, MLA_HEADS, per_head)[:, :, :keep]
    w = jnp.pad(w, ((0, 0), (0, 0), (0, HEAD_PAD - keep)))
    return w.reshape(k, MLA_HEADS * HEAD_PAD)


def _router_weights(w_group, b_group, w_expert, b_expert):
    d = w_group.shape[0]
    pad = ROUTE_LANES - N_GROUPS - N_EXPERTS
    wr = jnp.concatenate([w_group, w_expert, jnp.zeros((d, pad), F32)], axis=1).astype(BF16)
    br = jnp.concatenate([b_group, b_expert, jnp.zeros((pad,), F32)])[None, :]
    return wr, br


def kernel(x, c, positions, adaln_w, adaln_b, norm_mix_g, norm_ffn_g, ab_w_in, mla_q_norm_g, mla_w_uq,
           mla_kv_norm_g, mla_w_ukv, conv_w, ab_w_out, cd_w_in, sg_v_norm_g, sg_w_s, sg_b_s, gla_w_g2, gla_b_g,
           gla_norm_g, cd_w_out, moe_w_group, moe_b_group, moe_w_expert, moe_b_expert, moe_w_gate_up, moe_w_down,
           final_norm_g):
    b, s, d = x.shape
    n = b * s
    tiles_per_seq = s // ROW_TILE
    mod = _adaln_mod(c, adaln_w, adaln_b)
    rt = _rope_tables(positions)

    w = ab_w_in[0]
    q_w, kv_w, kr_w, b_w, c_w, x_w = jnp.split(w, [256, 384, 416, 928, 1440], axis=1)
    kr_w = jnp.pad(kr_w, ((0, 0), (MLA_NOPE, HEAD_PAD - MLA_NOPE - MLA_ROPE)))
    win0 = jnp.concatenate([q_w, kv_w, kr_w, b_w, c_w, x_w], axis=1).astype(BF16)
    wuq = _pad_heads(mla_w_uq[0], MLA_NOPE + MLA_ROPE, MLA_NOPE + MLA_ROPE).astype(BF16)
    wukv = mla_w_ukv[0].reshape(MLA_KV_RANK, MLA_HEADS, MLA_NOPE + MLA_V)
    wuk = _pad_heads(wukv[:, :, :MLA_NOPE].reshape(MLA_KV_RANK, -1), MLA_NOPE, MLA_NOPE).astype(BF16)
    wuv = _pad_heads(wukv[:, :, MLA_NOPE:].reshape(MLA_KV_RANK, -1), MLA_V, MLA_V).T.astype(BF16)
    q, k, v, bconv = _l0_in(x, mod[0], norm_mix_g[0][None], win0, mla_q_norm_g[0][None], wuq,
                            mla_kv_norm_g[0][None], wuk, wuv, rt, conv_w[0])
    a = _attention(q, k, v)
    wout = ab_w_out[0].astype(BF16)
    wr, br = _router_weights(moe_w_group[0], moe_b_group[0], moe_w_expert[0], moe_b_expert[0])
    x1, h2, route0, cnt0 = _outproj_router(a.reshape(n, -1), bconv.reshape(n, -1), x.reshape(n, d), mod[0],
                                           wout[:512], wout[512:], norm_ffn_g[0][None], wr, br, tiles_per_seq)
    mo0 = _moe(h2, route0, cnt0, moe_w_gate_up[0].astype(BF16), moe_w_down[0].astype(BF16))

    w = cd_w_in[0]
    u_w, v_w, gq_w, gk_w, gv_w, gl_w, gr_w = jnp.split(w, [512, 1024, 1280, 1536, 2048, 2064], axis=1)
    gl_w = jnp.pad(gl_w, ((0, 0), (0, 128 - GLA_GATE_RANK)))
    win1 = jnp.concatenate([u_w, v_w, gq_w, gk_w, gv_w, gr_w, gl_w], axis=1).astype(BF16)
    wg2 = jnp.pad(gla_w_g2[0], ((0, 128 - GLA_GATE_RANK), (0, 0))).astype(BF16)
    x2, cout, gq, gk, gv, gr, la = _l1_in(
        x1.reshape(b, s, d), mo0, mod[0], mod[1], norm_mix_g[1][None], win1, sg_v_norm_g[0][None], sg_w_s[0], sg_b_s[0].T, wg2, gla_b_g[0][None])
    dout = _gla(gq, gk, gv, gr, la, gla_norm_g[0][None])
    wout = cd_w_out[0].astype(BF16)
    wr, br = _router_weights(moe_w_group[1], moe_b_group[1], moe_w_expert[1], moe_b_expert[1])
    x3, h2, route1, cnt1 = _outproj_router(cout.reshape(n, -1), dout.reshape(n, -1), x2.reshape(n, d), mod[1],
                                           wout[:512], wout[512:], norm_ffn_g[1][None], wr, br, tiles_per_seq)
    mo1 = _moe(h2, route1, cnt1, moe_w_gate_up[1].astype(BF16), moe_w_down[1].astype(BF16))
    return _final(x3.reshape(b, s, d), mo1, mod[1], final_norm_g[None])
```
